```python
import jax, jax.numpy as jnp
from jax import lax
import numpy as np

D_MODEL = 2048
BATCH = 8
SEQ = 2048
DEPTH = 1

CHUNK = 64
N_MEM = 256

POOL_WINDOWS = (2, 4, 8, 16)
N_POOL_GROUPS = len(POOL_WINDOWS)
POOL_GROUP_DIM = D_MODEL // 16
POOL_WIDTH = N_POOL_GROUPS * POOL_GROUP_DIM
SB_HEAD_DIM = D_MODEL // 16
SB_HEADS = 8
SB_WIDTH = SB_HEADS * SB_HEAD_DIM
Q_BLOCK = 128
XA_HEAD_DIM = D_MODEL // 16
XA_HEADS = 4
XA_WIDTH = XA_HEADS * XA_HEAD_DIM
N_BRANCHES = 3
IN_SPLITS = (POOL_WIDTH, SB_WIDTH, SB_WIDTH, SB_WIDTH, XA_WIDTH, N_BRANCHES * D_MODEL)
IN_WIDTH = sum(IN_SPLITS)
N_EXPERTS = 32
TOP_K = 4
D_FF = D_MODEL
SWIGLU_LIMIT = 7.0
SWIGLU_ALPHA = 1.702
ROUTE_BLOCK = 128
EPS = 1e-5

kernel_name = "hybrid_pool_stickbreak_memxattn_moe"


def rmsnorm(x, g):
    xf = x.astype(jnp.float32)
    y = xf * lax.rsqrt(jnp.mean(xf * xf, axis=-1, keepdims=True) + EPS)
    return (y * g.astype(jnp.float32)).astype(x.dtype)


def pool_mixer(u, w_group, scale):
    B, S, _ = u.shape
    uf = u.astype(jnp.float32).reshape(B, S, N_POOL_GROUPS, POOL_GROUP_DIM)
    csum = jnp.cumsum(uf, axis=1)
    n_seen = jnp.arange(1, S + 1, dtype=jnp.float32)
    diffs = []
    for g, w in enumerate(POOL_WINDOWS):
        c = csum[:, :, g]
        lagged = jnp.pad(c[:, :S - w], ((0, 0), (w, 0), (0, 0)))
        mean = (c - lagged) / jnp.minimum(n_seen, float(w))[None, :, None]
        diffs.append(mean - uf[:, :, g])
    d = jnp.stack(diffs, axis=2).astype(u.dtype)
    y = jnp.einsum('bsgc,gcd->bsgd', d, w_group)
    return y.reshape(B, S, POOL_WIDTH) * scale


def stick_breaking_attention(q, k, v):
    B, S, H, Dh = q.shape
    scale = 1.0 / float(np.sqrt(Dh))
    outs = []
    for i in range(S // Q_BLOCK):
        t0, t1 = i * Q_BLOCK, (i + 1) * Q_BLOCK
        qb, kb, vb = q[:, t0:t1], k[:, :t1], v[:, :t1]
        z = jnp.einsum('bqhd,bkhd->bhqk', qb, kb).astype(jnp.float32) * scale
        strict = jnp.arange(t1)[None, :] < jnp.arange(t0, t1)[:, None]
        log_fail = jnp.where(strict, jax.nn.log_sigmoid(-z), 0.0)
        log_after = lax.cumsum(log_fail, axis=3, reverse=True) - log_fail
        a = jnp.where(strict, jnp.exp(jax.nn.log_sigmoid(z) + log_after), 0.0)
        outs.append(jnp.einsum('bhqk,bkhd->bqhd', a.astype(v.dtype), vb))
    return jnp.concatenate(outs, axis=1)


def memory_attention(q, mem_n, w_kv):
    B, S, _ = q.shape
    M = mem_n.shape[1]
    k, v = jnp.split(mem_n @ w_kv, 2, axis=-1)
    qh = q.reshape(B, S, XA_HEADS, XA_HEAD_DIM)
    kh = k.reshape(B, M, XA_HEADS, XA_HEAD_DIM)
    vh = v.reshape(B, M, XA_HEADS, XA_HEAD_DIM)
    s = jnp.einsum('bqhd,bmhd->bhqm', qh, kh).astype(jnp.float32) / float(np.sqrt(XA_HEAD_DIM))
    p = jax.nn.softmax(s, axis=-1).astype(v.dtype)
    return jnp.einsum('bhqm,bmhd->bqhd', p, vh).reshape(B, S, XA_WIDTH)


def clamped_swiglu(gate, up):
    gate = jnp.minimum(gate, SWIGLU_LIMIT)
    up = jnp.clip(up, -SWIGLU_LIMIT, SWIGLU_LIMIT)
    return gate * jax.nn.sigmoid(SWIGLU_ALPHA * gate) * (up + 1.0)


def moe_ffn(xn, w_router, b_router, w_gate, b_gate, w_up, b_up, w_down, b_down):
    T, D = xn.shape
    TK = T * TOP_K
    logits = (xn @ w_router + b_router).astype(jnp.float32)
    top_val, top_idx = lax.top_k(logits, TOP_K)
    probs = jax.nn.softmax(top_val, axis=-1)
    e_flat = top_idx.reshape(-1)
    tok_flat = jnp.broadcast_to(jnp.arange(T, dtype=jnp.int32)[:, None], (T, TOP_K)).reshape(-1)
    g_flat = probs.reshape(-1)
    order = jnp.argsort(e_flat)
    e_s, tok_s, g_s = e_flat[order], tok_flat[order], g_flat[order]
    counts = jnp.bincount(e_flat, length=N_EXPERTS)
    padded = (counts + ROUTE_BLOCK - 1) // ROUTE_BLOCK * ROUTE_BLOCK
    grp_start = jnp.cumsum(counts) - counts
    pad_end = jnp.cumsum(padded)
    pad_start = pad_end - padded
    dest = pad_start[e_s] + (jnp.arange(TK) - grp_start[e_s])
    n_rows = TK + N_EXPERTS * ROUTE_BLOCK
    n_blk = n_rows // ROUTE_BLOCK
    rows_tok = jnp.full((n_rows,), T, jnp.int32).at[dest].set(tok_s)
    rows_gate = jnp.zeros((n_rows,), jnp.float32).at[dest].set(g_s)
    blk_expert = jnp.minimum(
        jnp.searchsorted(pad_end, jnp.arange(n_blk) * ROUTE_BLOCK, side='right'), N_EXPERTS - 1)
    x_pad = jnp.concatenate([xn, jnp.zeros((1, D), xn.dtype)], axis=0)

    def run_block(args):
        tok, e = args
        xb = x_pad[tok]
        h = clamped_swiglu(xb @ w_gate[e] + b_gate[e], xb @ w_up[e] + b_up[e])
        return h @ w_down[e] + b_down[e]

    y = lax.map(run_block, (rows_tok.reshape(n_blk, ROUTE_BLOCK), blk_expert))
    y = y.reshape(n_rows, D) * rows_gate.astype(xn.dtype)[:, None]
    return jnp.zeros((T + 1, D), xn.dtype).at[rows_tok].add(y)[:T]


def setup_inputs(seed: int = 0) -> dict:
    key = jax.random.key(seed)
    ks = jax.random.split(key, 24)
    f32 = jnp.float32

    def nrm(k, shape, scale):
        return jax.random.normal(k, shape, f32) * scale

    L = DEPTH
    return {
        "x": nrm(ks[0], (BATCH, SEQ, D_MODEL), 1.0),
        "mem": nrm(ks[1], (BATCH, N_MEM, D_MODEL), 1.0),
        "norm_mix_g": 1.0 + nrm(ks[2], (L, D_MODEL), 0.05),
        "norm_mem_g": 1.0 + nrm(ks[3], (L, D_MODEL), 0.05),
        "w_in": nrm(ks[4], (L, D_MODEL, IN_WIDTH), D_MODEL ** -0.5),
        "w_pool_group": nrm(ks[5], (L, N_POOL_GROUPS, POOL_GROUP_DIM, POOL_GROUP_DIM), POOL_GROUP_DIM ** -0.5),
        "pool_scale": 1.0 + nrm(ks[6], (L, POOL_WIDTH), 0.05),
        "w_pool_proj": nrm(ks[7], (L, POOL_WIDTH, D_MODEL), POOL_WIDTH ** -0.5),
        "w_sb_proj": nrm(ks[8], (L, SB_WIDTH, D_MODEL), SB_WIDTH ** -0.5),
        "w_mem_kv": nrm(ks[9], (L, D_MODEL, 2 * XA_WIDTH), D_MODEL ** -0.5),
        "w_mem_proj": nrm(ks[10], (L, XA_WIDTH, D_MODEL), XA_WIDTH ** -0.5),
        "w_out": nrm(ks[11], (L, D_MODEL, D_MODEL), D_MODEL ** -0.5),
        "norm_ffn_g": 1.0 + nrm(ks[12], (L, D_MODEL), 0.05),
        "w_router": nrm(ks[13], (L, D_MODEL, N_EXPERTS), D_MODEL ** -0.5),
        "b_router": nrm(ks[14], (L, N_EXPERTS), 0.01),
        "w_gate": nrm(ks[15], (L, N_EXPERTS, D_MODEL, D_FF), D_MODEL ** -0.5),
        "b_gate": nrm(ks[16], (L, N_EXPERTS, D_FF), 0.01),
        "w_up": nrm(ks[17], (L, N_EXPERTS, D_MODEL, D_FF), D_MODEL ** -0.5),
        "b_up": nrm(ks[18], (L, N_EXPERTS, D_FF), 0.01),
        "w_down": nrm(ks[19], (L, N_EXPERTS, D_FF, D_MODEL), D_FF ** -0.5),
        "b_down": nrm(ks[20], (L, N_EXPERTS, D_MODEL), 0.01),
        "norm_final_g": 1.0 + nrm(ks[21], (D_MODEL,), 0.05),
    }


def reference(x, mem, norm_mix_g, norm_mem_g, w_in, w_pool_group, pool_scale, w_pool_proj,
              w_sb_proj, w_mem_kv, w_mem_proj, w_out, norm_ffn_g, w_router, b_router,
              w_gate, b_gate, w_up, b_up, w_down, b_down, norm_final_g):
    B, S, D = x.shape
    split_at = [int(v) for v in np.cumsum(IN_SPLITS)[:-1]]
    h = x
    for l in range(DEPTH):
        xn = rmsnorm(h, norm_mix_g[l])
        u_pool, q_sb, k_sb, v_sb, q_xa, gate_logits = jnp.split(xn @ w_in[l], split_at, axis=-1)
        y_pool = pool_mixer(u_pool, w_pool_group[l], pool_scale[l]) @ w_pool_proj[l]
        heads = lambda t: t.reshape(B, S, SB_HEADS, SB_HEAD_DIM)
        y_sb = stick_breaking_attention(heads(q_sb), heads(k_sb), heads(v_sb)).reshape(B, S, SB_WIDTH)
        y_sb = y_sb @ w_sb_proj[l]
        mem_n = rmsnorm(mem, norm_mem_g[l])
        y_xa = memory_attention(q_xa, mem_n, w_mem_kv[l]) @ w_mem_proj[l]
        gates = jax.nn.sigmoid(gate_logits.astype(jnp.float32)).astype(h.dtype)
        gates = gates.reshape(B, S, N_BRANCHES, D)
        merged = gates[:, :, 0] * y_pool + gates[:, :, 1] * y_sb + gates[:, :, 2] * y_xa
        h = h + merged @ w_out[l]
        hn = rmsnorm(h, norm_ffn_g[l]).reshape(B * S, D)
        h = h + moe_ffn(hn, w_router[l], b_router[l], w_gate[l], b_gate[l], w_up[l], b_up[l],
                        w_down[l], b_down[l]).reshape(B, S, D)
    return rmsnorm(h, norm_final_g)
```

```python
import functools

import numpy as np
import jax
import jax.numpy as jnp
from jax import lax
from jax.experimental import pallas as pl
from jax.experimental.pallas import tpu as pltpu

EPS = 1e-5
HEAD_DIM = 128
POOL_WINDOWS = (2, 4, 8, 16)
POOL_HALO = 16
TOP_K = 4
SWIGLU_LIMIT = 7.0
SWIGLU_ALPHA = 1.702
LANES = 128

BF16 = jnp.bfloat16
F32 = jnp.float32

MIB = 1024 * 1024


def _params(sem, vmem_mib):
    return pltpu.CompilerParams(dimension_semantics=sem, vmem_limit_bytes=vmem_mib * MIB)


def _rms(x, g):
    ms = jnp.mean(x * x, axis=-1, keepdims=True)
    return x * lax.rsqrt(ms + EPS) * g


def _norm_proj_kernel(x_ref, g_ref, w_ref, o_ref, xn_ref, *, act):
    @pl.when(pl.program_id(1) == 0)
    def _():
        xn_ref[...] = _rms(x_ref[...], g_ref[...]).astype(BF16)

    y = jnp.dot(xn_ref[...], w_ref[...], preferred_element_type=F32)
    if act == "sigmoid":
        y = jax.nn.sigmoid(y)
    o_ref[...] = y.astype(o_ref.dtype)


def norm_proj(x, g, w, out_dtype, act=None, tm=512, tn=512):
    R, D = x.shape
    N = w.shape[1]
    tm = min(tm, R)
    tn = min(tn, N)
    assert R % tm == 0 and N % tn == 0
    return pl.pallas_call(
        functools.partial(_norm_proj_kernel, act=act),
        grid=(R // tm, N // tn),
        in_specs=[
            pl.BlockSpec((tm, D), lambda i, j: (i, 0)),
            pl.BlockSpec((1, D), lambda i, j: (0, 0)),
            pl.BlockSpec((D, tn), lambda i, j: (0, j)),
        ],
        out_specs=pl.BlockSpec((tm, tn), lambda i, j: (i, j)),
        out_shape=jax.ShapeDtypeStruct((R, N), out_dtype),
        scratch_shapes=[pltpu.VMEM((tm, D), BF16)],
        compiler_params=_params(("arbitrary", "arbitrary"), 40),
        name="norm_proj",
    )(x, g.reshape(1, D), w)


def _pool_kernel(u_ref, wg_ref, sc_ref, o_ref, pad_ref, *, S):
    W = u_ref.shape[1]
    pad_ref[0:POOL_HALO, :] = jnp.zeros((POOL_HALO, W), F32)
    pad_ref[POOL_HALO:POOL_HALO + S, :] = u_ref[...]
    t = lax.broadcasted_iota(jnp.int32, (S, 1), 0).astype(F32)
    for g, w in enumerate(POOL_WINDOWS):
        lo, hi = g * HEAD_DIM, (g + 1) * HEAD_DIM
        u = pad_ref[POOL_HALO:POOL_HALO + S, lo:hi]
        acc = u
        for k in range(1, w):
            acc = acc + pad_ref[POOL_HALO - k:POOL_HALO - k + S, lo:hi]
        mean = acc / jnp.minimum(t + 1.0, float(w))
        d = (mean - u).astype(BF16)
        y = jnp.dot(d, wg_ref[g], preferred_element_type=F32) * sc_ref[:, lo:hi]
        o_ref[:, lo:hi] = y.astype(o_ref.dtype)


def pool_mixer(u, w_group, scale, B, S):
    T, W = u.shape
    G = len(POOL_WINDOWS)
    assert W == G * HEAD_DIM
    return pl.pallas_call(
        functools.partial(_pool_kernel, S=S),
        grid=(B,),
        in_specs=[
            pl.BlockSpec((S, W), lambda b: (b, 0)),
            pl.BlockSpec((G, HEAD_DIM, HEAD_DIM), lambda b: (0, 0, 0)),
            pl.BlockSpec((1, W), lambda b: (0, 0)),
        ],
        out_specs=pl.BlockSpec((S, W), lambda b: (b, 0)),
        out_shape=jax.ShapeDtypeStruct((T, W), BF16),
        scratch_shapes=[pltpu.VMEM((S + POOL_HALO, W), F32)],
        compiler_params=_params(("arbitrary",), 40),
        name="pool_mixer",
    )(u, w_group.astype(BF16), scale.reshape(1, W))


def _sb_kernel(q_ref, k_ref, v_ref, uu_ref, o_ref, *, tq, scale):
    qi = pl.program_id(2)
    q = q_ref[...]
    uu = uu_ref[...]
    row = qi * tq + lax.broadcasted_iota(jnp.int32, (tq, tq), 0)
    col0 = lax.broadcasted_iota(jnp.int32, (tq, tq), 1)

    def body(it, carry):
        acc, run = carry
        kb = qi - it
        off = pl.multiple_of(kb * tq, tq)
        kblk = k_ref[pl.ds(off, tq), :]
        vblk = v_ref[pl.ds(off, tq), :]
        z = lax.dot_general(q, kblk, (((1,), (1,)), ((), ())), preferred_element_type=F32) * scale
        strict = (col0 + kb * tq) < row
        soft = jnp.log(1.0 + jnp.exp(-jnp.abs(z)))
        log_fail = jnp.where(strict, -jnp.maximum(z, 0.0) - soft, 0.0)
        log_hit = jnp.minimum(z, 0.0) - soft
        hi = log_fail.astype(BF16)
        lo = (log_fail - hi.astype(F32)).astype(BF16)
        after = jnp.dot(jnp.concatenate([hi, lo], axis=1), uu, preferred_element_type=F32)
        a = jnp.where(strict, jnp.exp(log_hit + after + run), 0.0)
        acc = acc + jnp.dot(a.astype(BF16), vblk, preferred_element_type=F32)
        run = run + jnp.sum(log_fail, axis=1, keepdims=True)
        return acc, run

    acc, _ = lax.fori_loop(
        0, qi + 1, body, (jnp.zeros((tq, HEAD_DIM), F32), jnp.zeros((tq, 1), F32)))
    o_ref[...] = acc.astype(o_ref.dtype)


def sb_attention(qkv, B, S, n_heads, tq=256):
    T = qkv.shape[0]
    assert S % tq == 0
    nq = S // tq
    tri = (np.arange(tq)[:, None] > np.arange(tq)[None, :]).astype(np.float32)
    uu = jnp.asarray(np.concatenate([tri, tri], axis=0), BF16)
    return pl.pallas_call(
        functools.partial(_sb_kernel, tq=tq, scale=1.0 / float(np.sqrt(HEAD_DIM))),
        grid=(B, n_heads, nq),
        in_specs=[
            pl.BlockSpec((tq, HEAD_DIM), lambda b, h, i: (b * nq + i, h)),
            pl.BlockSpec((S, HEAD_DIM), lambda b, h, i: (b, n_heads + h)),
            pl.BlockSpec((S, HEAD_DIM), lambda b, h, i: (b, 2 * n_heads + h)),
            pl.BlockSpec((2 * tq, tq), lambda b, h, i: (0, 0)),
        ],
        out_specs=pl.BlockSpec((tq, HEAD_DIM), lambda b, h, i: (b * nq + i, h)),
        out_shape=jax.ShapeDtypeStruct((T, n_heads * HEAD_DIM), BF16),
        compiler_params=_params(("arbitrary", "arbitrary", "arbitrary"), 40),
        name="sb_attention",
    )(qkv, qkv, qkv, uu)


def _xattn_kernel(q_ref, kv_ref, o_ref, *, n_heads, scale):
    W = n_heads * HEAD_DIM
    for h in range(n_heads):
        lo, hi = h * HEAD_DIM, (h + 1) * HEAD_DIM
        q = q_ref[:, lo:hi]
        k = kv_ref[:, lo:hi]
        v = kv_ref[:, W + lo:W + hi]
        s = lax.dot_general(q, k, (((1,), (1,)), ((), ())), preferred_element_type=F32) * scale
        m = jnp.max(s, axis=-1, keepdims=True)
        p = jnp.exp(s - m)
        p = p / jnp.sum(p, axis=-1, keepdims=True)
        o_ref[:, lo:hi] = jnp.dot(p.astype(BF16), v, preferred_element_type=F32).astype(o_ref.dtype)


def mem_attention(qkv, memkv, S, q_col_block, n_heads, tq=512):
    T = qkv.shape[0]
    W = n_heads * HEAD_DIM
    M = memkv.shape[0] // (T // S)
    assert S % tq == 0
    per = S // tq
    return pl.pallas_call(
        functools.partial(_xattn_kernel, n_heads=n_heads, scale=1.0 / float(np.sqrt(HEAD_DIM))),
        grid=(T // tq,),
        in_specs=[
            pl.BlockSpec((tq, W), lambda i: (i, q_col_block)),
            pl.BlockSpec((M, 2 * W), lambda i: (i // per, 0)),
        ],
        out_specs=pl.BlockSpec((tq, W), lambda i: (i, 0)),
        out_shape=jax.ShapeDtypeStruct((T, W), BF16),
        compiler_params=_params(("arbitrary",), 40),
        name="mem_attention",
    )(qkv, memkv)


def _merge_kernel(p_ref, sb_ref, xa_ref, g_ref, x_ref, wpp_ref, wsp_ref, wmp_ref, wout_ref,
                  gffn_ref, wr_ref, br_ref, h_ref, hn_ref, lg_ref):
    D = x_ref.shape[1]
    y_pool = jnp.dot(p_ref[...], wpp_ref[...], preferred_element_type=F32)
    y_sb = jnp.dot(sb_ref[...], wsp_ref[...], preferred_element_type=F32)
    y_xa = jnp.dot(xa_ref[...], wmp_ref[...], preferred_element_type=F32)
    merged = g_ref[:, 0:D] * y_pool + g_ref[:, D:2 * D] * y_sb + g_ref[:, 2 * D:3 * D] * y_xa
    h = x_ref[...] + jnp.dot(merged.astype(BF16), wout_ref[...], preferred_element_type=F32)
    h_ref[...] = h
    hn = _rms(h, gffn_ref[...])
    hn_ref[...] = hn
    lg_ref[...] = jnp.dot(hn.astype(BF16), wr_ref[...], preferred_element_type=F32) + br_ref[...]


def merge_block(p, a_sb, a_xa, gates, x, wpp, wsp, wmp, wout, g_ffn, w_router, b_router, tm=256):
    T, D = x.shape
    E = w_router.shape[1]
    wr = jnp.zeros((D, LANES), BF16).at[:, :E].set(w_router.astype(BF16))
    br = jnp.zeros((1, LANES), F32).at[0, :E].set(b_router)
    row = lambda w: pl.BlockSpec((tm, w), lambda i: (i, 0))
    full = lambda a: pl.BlockSpec(a.shape, lambda i: (0,) * a.ndim, pipeline_mode=pl.Buffered(1))
    weights = [wpp.astype(BF16), wsp.astype(BF16), wmp.astype(BF16), wout.astype(BF16),
               g_ffn.reshape(1, D), wr, br]
    return pl.pallas_call(
        _merge_kernel,
        grid=(T // tm,),
        in_specs=[row(p.shape[1]), row(a_sb.shape[1]), row(a_xa.shape[1]), row(3 * D), row(D)]
                 + [full(a) for a in weights],
        out_specs=[row(D), row(D), row(LANES)],
        out_shape=[jax.ShapeDtypeStruct((T, D), F32), jax.ShapeDtypeStruct((T, D), F32),
                   jax.ShapeDtypeStruct((T, LANES), F32)],
        compiler_params=_params(("arbitrary",), 56),
        name="merge_block",
    )(p, a_sb, a_xa, gates, x, *weights)


def _route_kernel(lg_ref, tri_ref, idx_ref, prob_ref, rank_ref, cnt_ref, carry_ref, *, n_experts):
    @pl.when(pl.program_id(0) == 0)
    def _():
        carry_ref[...] = jnp.zeros_like(carry_ref)

    tr = lg_ref.shape[0]
    lane = lax.broadcasted_iota(jnp.int32, (tr, LANES), 1)
    logit = jnp.where(lane < n_experts, lg_ref[...], -jnp.inf)
    onehot = jnp.zeros((tr, LANES), F32)
    vals, idxs = [], []
    for _ in range(TOP_K):
        m = jnp.max(logit, axis=-1, keepdims=True)
        ik = jnp.min(jnp.where(logit == m, lane, LANES), axis=-1, keepdims=True)
        sel = lane == ik
        onehot = jnp.where(sel, 1.0, onehot)
        logit = jnp.where(sel, -jnp.inf, logit)
        vals.append(m)
        idxs.append(ik)
    es = [jnp.exp(v - vals[0]) for v in vals]
    den = es[0] + es[1] + es[2] + es[3]
    before = jnp.dot(tri_ref[...], onehot.astype(BF16), preferred_element_type=F32) + carry_ref[...]
    carry_ref[...] = carry_ref[...] + jnp.sum(onehot, axis=0, keepdims=True)
    cnt_ref[...] = carry_ref[...]
    idx_out = jnp.zeros((tr, LANES), jnp.int32)
    prob_out = jnp.zeros((tr, LANES), F32)
    rank_out = jnp.zeros((tr, LANES), F32)
    for k in range(TOP_K):
        rk = jnp.sum(jnp.where(lane == idxs[k], before, 0.0), axis=-1, keepdims=True)
        idx_out = jnp.where(lane == k, idxs[k], idx_out)
        prob_out = jnp.where(lane == k, es[k] / den, prob_out)
        rank_out = jnp.where(lane == k, rk, rank_out)
    idx_ref[...] = idx_out
    prob_ref[...] = prob_out
    rank_ref[...] = rank_out.astype(jnp.int32)


def route(logits, n_experts, tr=256):
    T = logits.shape[0]
    tri = jnp.asarray((np.arange(tr)[None, :] < np.arange(tr)[:, None]).astype(np.float32), BF16)
    row = pl.BlockSpec((tr, LANES), lambda i: (i, 0))
    return pl.pallas_call(
        functools.partial(_route_kernel, n_experts=n_experts),
        grid=(T // tr,),
        in_specs=[row, pl.BlockSpec((tr, tr), lambda i: (0, 0))],
        out_specs=[row, row, row, pl.BlockSpec((1, LANES), lambda i: (0, 0))],
        out_shape=[jax.ShapeDtypeStruct((T, LANES), jnp.int32), jax.ShapeDtypeStruct((T, LANES), F32),
                   jax.ShapeDtypeStruct((T, LANES), jnp.int32), jax.ShapeDtypeStruct((1, LANES), F32)],
        scratch_shapes=[pltpu.VMEM((1, LANES), F32)],
        compiler_params=_params(("arbitrary",), 32),
        name="route",
    )(logits, tri)


def _gather_kernel(tok_ref, nu_ref, src_ref, o_ref, buf_ref, sem, *, tm):
    i = pl.program_id(0)

    @pl.when(i < nu_ref[0])
    def _():
        def issue(r, c):
            t = tok_ref[i * tm + r]
            pltpu.make_async_copy(src_ref.at[pl.ds(t, 1)], buf_ref.at[pl.ds(r, 1)], sem).start()
            return c

        lax.fori_loop(0, tm, issue, 0)
        pltpu.make_async_copy(src_ref.at[pl.ds(0, tm)], buf_ref, sem).wait()
        o_ref[...] = buf_ref[...].astype(o_ref.dtype)

    @pl.when(i >= nu_ref[0])
    def _():
        o_ref[...] = jnp.zeros_like(o_ref)


def gather_rows(rows_tok, n_used, src, tm):
    n_rows = rows_tok.shape[0]
    D = src.shape[1]
    return pl.pallas_call(
        functools.partial(_gather_kernel, tm=tm),
        grid_spec=pltpu.PrefetchScalarGridSpec(
            num_scalar_prefetch=2,
            grid=(n_rows // tm,),
            in_specs=[pl.BlockSpec(memory_space=pl.ANY)],
            out_specs=pl.BlockSpec((tm, D), lambda i, tok, nu: (i, 0)),
            scratch_shapes=[pltpu.VMEM((tm, D), F32), pltpu.SemaphoreType.DMA(())],
        ),
        out_shape=jax.ShapeDtypeStruct((n_rows, D), BF16),
        compiler_params=_params(("arbitrary",), 32),
        name="gather_rows",
    )(rows_tok, n_used, src)


def _expert_changed(i, be_ref):
    return jnp.logical_or(i == 0, be_ref[i] != be_ref[jnp.maximum(i - 1, 0)])


def _moe_up_kernel(be_ref, nu_ref, x_ref, wg_ref, wu_ref, bg_ref, bu_ref, o_ref, wg_bf, wu_bf):
    i = pl.program_id(1)

    @pl.when(i < nu_ref[0])
    def _():
        @pl.when(_expert_changed(i, be_ref))
        def _():
            wg_bf[...] = wg_ref[0].astype(BF16)
            wu_bf[...] = wu_ref[0].astype(BF16)

        x = x_ref[...]
        gate = jnp.dot(x, wg_bf[...], preferred_element_type=F32) + bg_ref[0]
        up = jnp.dot(x, wu_bf[...], preferred_element_type=F32) + bu_ref[0]
        gate = jnp.minimum(gate, SWIGLU_LIMIT)
        up = jnp.clip(up, -SWIGLU_LIMIT, SWIGLU_LIMIT)
        o_ref[...] = (gate * jax.nn.sigmoid(SWIGLU_ALPHA * gate) * (up + 1.0)).astype(o_ref.dtype)

    @pl.when(i >= nu_ref[0])
    def _():
        o_ref[...] = jnp.zeros_like(o_ref)


def _moe_down_kernel(be_ref, nu_ref, h_ref, wd_ref, bd_ref, o_ref, wd_bf):
    i = pl.program_id(1)

    @pl.when(i < nu_ref[0])
    def _():
        @pl.when(_expert_changed(i, be_ref))
        def _():
            wd_bf[...] = wd_ref[0].astype(BF16)

        o_ref[...] = jnp.dot(h_ref[...], wd_bf[...], preferred_element_type=F32) + bd_ref[0]

    @pl.when(i >= nu_ref[0])
    def _():
        o_ref[...] = jnp.zeros_like(o_ref)


def _blk(i, nu):
    return jnp.minimum(i, nu[0] - 1)


def moe_up(blk_expert, n_used, xs, w_gate, b_gate, w_up, b_up, tm, tn=512):
    n_rows, D = xs.shape
    E, _, F = w_gate.shape
    tn = min(tn, F)
    w_spec = pl.BlockSpec((1, D, tn), lambda j, i, be, nu: (be[_blk(i, nu)], 0, j))
    b_spec = pl.BlockSpec((1, 1, tn), lambda j, i, be, nu: (be[_blk(i, nu)], 0, j))
    return pl.pallas_call(
        _moe_up_kernel,
        grid_spec=pltpu.PrefetchScalarGridSpec(
            num_scalar_prefetch=2,
            grid=(F // tn, n_rows // tm),
            in_specs=[pl.BlockSpec((tm, D), lambda j, i, be, nu: (_blk(i, nu), 0)),
                      w_spec, w_spec, b_spec, b_spec],
            out_specs=pl.BlockSpec((tm, tn), lambda j, i, be, nu: (i, j)),
            scratch_shapes=[pltpu.VMEM((D, tn), BF16), pltpu.VMEM((D, tn), BF16)],
        ),
        out_shape=jax.ShapeDtypeStruct((n_rows, F), BF16),
        compiler_params=_params(("arbitrary", "arbitrary"), 48),
        name="moe_up",
    )(blk_expert, n_used, xs, w_gate, w_up, b_gate.reshape(E, 1, F), b_up.reshape(E, 1, F))


def moe_down(blk_expert, n_used, hs, w_down, b_down, tm, tn=512):
    n_rows, F = hs.shape
    E, _, D = w_down.shape
    tn = min(tn, D)
    return pl.pallas_call(
        _moe_down_kernel,
        grid_spec=pltpu.PrefetchScalarGridSpec(
            num_scalar_prefetch=2,
            grid=(D // tn, n_rows // tm),
            in_specs=[pl.BlockSpec((tm, F), lambda j, i, be, nu: (_blk(i, nu), 0)),
                      pl.BlockSpec((1, F, tn), lambda j, i, be, nu: (be[_blk(i, nu)], 0, j)),
                      pl.BlockSpec((1, 1, tn), lambda j, i, be, nu: (be[_blk(i, nu)], 0, j))],
            out_specs=pl.BlockSpec((tm, tn), lambda j, i, be, nu: (i, j)),
            scratch_shapes=[pltpu.VMEM((F, tn), BF16)],
        ),
        out_shape=jax.ShapeDtypeStruct((n_rows, D), F32),
        compiler_params=_params(("arbitrary", "arbitrary"), 48),
        name="moe_down",
    )(blk_expert, n_used, hs, w_down, b_down.reshape(E, 1, D))


def _combine_kernel(dest_ref, y_ref, p_ref, h_ref, g_ref, o_ref, buf_ref, sem, *, tc, final_norm):
    i = pl.program_id(0)

    def issue(r, c):
        for k in range(TOP_K):
            d = dest_ref[(i * tc + r) * TOP_K + k]
            pltpu.make_async_copy(y_ref.at[pl.ds(d, 1)], buf_ref.at[k, pl.ds(r, 1)], sem).start()
        return c

    lax.fori_loop(0, tc, issue, 0)
    for k in range(TOP_K):
        pltpu.make_async_copy(y_ref.at[pl.ds(0, tc)], buf_ref.at[k], sem).wait()
    p = p_ref[...]
    h = h_ref[...]
    for k in range(TOP_K):
        h = h + p[:, k:k + 1] * buf_ref[k]
    o_ref[...] = _rms(h, g_ref[...]) if final_norm else h


def combine(dest_flat, y, probs, h, g_final, final_norm, tc=128):
    T, D = h.shape
    return pl.pallas_call(
        functools.partial(_combine_kernel, tc=tc, final_norm=final_norm),
        grid_spec=pltpu.PrefetchScalarGridSpec(
            num_scalar_prefetch=1,
            grid=(T // tc,),
            in_specs=[pl.BlockSpec(memory_space=pl.ANY),
                      pl.BlockSpec((tc, LANES), lambda i, d: (i, 0)),
                      pl.BlockSpec((tc, D), lambda i, d: (i, 0)),
                      pl.BlockSpec((1, D), lambda i, d: (0, 0))],
            out_specs=pl.BlockSpec((tc, D), lambda i, d: (i, 0)),
            scratch_shapes=[pltpu.VMEM((TOP_K, tc, D), F32), pltpu.SemaphoreType.DMA(())],
        ),
        out_shape=jax.ShapeDtypeStruct((T, D), F32),
        compiler_params=_params(("arbitrary",), 32),
        name="combine",
    )(dest_flat, y, probs, h, g_final.reshape(1, D))


MOE_ROW_BLOCK = 256


def kernel(x, mem, norm_mix_g, norm_mem_g, w_in, w_pool_group, pool_scale, w_pool_proj, w_sb_proj,
           w_mem_kv, w_mem_proj, w_out, norm_ffn_g, w_router, b_router, w_gate, b_gate, w_up, b_up,
           w_down, b_down, norm_final_g):
    B, S, D = x.shape
    M = mem.shape[1]
    T = B * S
    depth = w_in.shape[0]
    pool_w = w_pool_proj.shape[1]
    sb_w = w_sb_proj.shape[1]
    xa_w = w_mem_proj.shape[1]
    E = w_router.shape[2]
    sb_heads = sb_w // HEAD_DIM
    xa_heads = xa_w // HEAD_DIM
    qkv_w = 3 * sb_w + xa_w
    assert (3 * sb_w) % xa_w == 0

    h = x.reshape(T, D)
    mem2 = mem.reshape(B * M, D)
    for l in range(depth):
        w_in_bf = w_in[l].astype(BF16)
        u_pool = norm_proj(h, norm_mix_g[l], w_in_bf[:, :pool_w], F32)
        qkv = norm_proj(h, norm_mix_g[l], w_in_bf[:, pool_w:pool_w + qkv_w], BF16)
        gates = norm_proj(h, norm_mix_g[l], w_in_bf[:, pool_w + qkv_w:], F32, act="sigmoid")
        memkv = norm_proj(mem2, norm_mem_g[l], w_mem_kv[l].astype(BF16), BF16)
        p = pool_mixer(u_pool, w_pool_group[l], pool_scale[l], B, S)
        a_sb = sb_attention(qkv, B, S, sb_heads)
        a_xa = mem_attention(qkv, memkv, S, (3 * sb_w) // xa_w, xa_heads)
        h, hn, logits = merge_block(p, a_sb, a_xa, gates, h, w_pool_proj[l], w_sb_proj[l],
                                    w_mem_proj[l], w_out[l], norm_ffn_g[l], w_router[l], b_router[l])
        idx, probs, rank, counts = route(logits, E)
        tm = MOE_ROW_BLOCK
        n_blk = (T * TOP_K) // tm + E
        counts = counts[0, :E].astype(jnp.int32)
        padded = (counts + tm - 1) // tm * tm
        pad_end = jnp.cumsum(padded)
        pad_start = pad_end - padded
        dest = pad_start[idx[:, :TOP_K]] + rank[:, :TOP_K]
        tok = jnp.broadcast_to(jnp.arange(T, dtype=jnp.int32)[:, None], (T, TOP_K))
        rows_tok = jnp.zeros((n_blk * tm,), jnp.int32).at[dest.reshape(-1)].set(tok.reshape(-1))
        blk_expert = jnp.minimum(
            jnp.searchsorted(pad_end, jnp.arange(n_blk, dtype=jnp.int32) * tm, side="right"),
            E - 1).astype(jnp.int32)
        n_used = (pad_end[-1:] // tm).astype(jnp.int32)
        xs = gather_rows(rows_tok, n_used, hn, tm)
        hs = moe_up(blk_expert, n_used, xs, w_gate[l], b_gate[l], w_up[l], b_up[l], tm)
        ys = moe_down(blk_expert, n_used, hs, w_down[l], b_down[l], tm)
        h = combine(dest.reshape(-1).astype(jnp.int32), ys, probs, h, norm_final_g,
                    final_norm=(l == depth - 1))
    return h.reshape(B, S, D)
```

```python
import functools

import numpy as np
import jax
import jax.numpy as jnp
from jax import lax
from jax.experimental import pallas as pl
from jax.experimental.pallas import tpu as pltpu

EPS = 1e-5
HEAD_DIM = 128
POOL_WINDOWS = (2, 4, 8, 16)
POOL_HALO = 16
TOP_K = 4
SWIGLU_LIMIT = 7.0
SWIGLU_ALPHA = 1.702
LANES = 128

BF16 = jnp.bfloat16
F32 = jnp.float32

MIB = 1024 * 1024


def _params(sem, vmem_mib, row_dma=False):
    return pltpu.CompilerParams(dimension_semantics=sem, vmem_limit_bytes=vmem_mib * MIB,
                                disable_bounds_checks=row_dma)


def _rms(x, g):
    ms = jnp.mean(x * x, axis=-1, keepdims=True)
    return x * lax.rsqrt(ms + EPS) * g


def _norm_proj_kernel(x_ref, g_ref, w_ref, o_ref, xn_ref, *, act):
    @pl.when(pl.program_id(1) == 0)
    def _():
        xn_ref[...] = _rms(x_ref[...], g_ref[...]).astype(BF16)

    y = jnp.dot(xn_ref[...], w_ref[...], preferred_element_type=F32)
    if act == "sigmoid":
        y = jax.nn.sigmoid(y)
    o_ref[...] = y.astype(o_ref.dtype)


def norm_proj(x, g, w, out_dtype, act=None, tm=1024, tn=1024):
    R, D = x.shape
    N = w.shape[1]
    tm = min(tm, R)
    tn = min(tn, N)
    assert R % tm == 0 and N % tn == 0
    return pl.pallas_call(
        functools.partial(_norm_proj_kernel, act=act),
        grid=(R // tm, N // tn),
        in_specs=[
            pl.BlockSpec((tm, D), lambda i, j: (i, 0)),
            pl.BlockSpec((1, D), lambda i, j: (0, 0)),
            pl.BlockSpec((D, tn), lambda i, j: (0, j)),
        ],
        out_specs=pl.BlockSpec((tm, tn), lambda i, j: (i, j)),
        out_shape=jax.ShapeDtypeStruct((R, N), out_dtype),
        scratch_shapes=[pltpu.VMEM((tm, D), BF16)],
        compiler_params=_params(("arbitrary", "arbitrary"), 48),
        name="norm_proj",
    )(x, g.reshape(1, D), w)


def _pool_kernel(u_ref, wg_ref, sc_ref, o_ref, pad_ref, *, S):
    W = u_ref.shape[1]
    pad_ref[0:POOL_HALO, :] = jnp.zeros((POOL_HALO, W), F32)
    pad_ref[POOL_HALO:POOL_HALO + S, :] = u_ref[...]
    t = lax.broadcasted_iota(jnp.int32, (S, 1), 0).astype(F32)
    for g, w in enumerate(POOL_WINDOWS):
        lo, hi = g * HEAD_DIM, (g + 1) * HEAD_DIM
        u = pad_ref[POOL_HALO:POOL_HALO + S, lo:hi]
        acc = u
        for k in range(1, w):
            acc = acc + pad_ref[POOL_HALO - k:POOL_HALO - k + S, lo:hi]
        mean = acc / jnp.minimum(t + 1.0, float(w))
        d = (mean - u).astype(BF16)
        y = jnp.dot(d, wg_ref[g], preferred_element_type=F32) * sc_ref[:, lo:hi]
        o_ref[:, lo:hi] = y.astype(o_ref.dtype)


def pool_mixer(u, w_group, scale, B, S):
    T, W = u.shape
    G = len(POOL_WINDOWS)
    assert W == G * HEAD_DIM
    return pl.pallas_call(
        functools.partial(_pool_kernel, S=S),
        grid=(B,),
        in_specs=[
            pl.BlockSpec((S, W), lambda b: (b, 0)),
            pl.BlockSpec((G, HEAD_DIM, HEAD_DIM), lambda b: (0, 0, 0)),
            pl.BlockSpec((1, W), lambda b: (0, 0)),
        ],
        out_specs=pl.BlockSpec((S, W), lambda b: (b, 0)),
        out_shape=jax.ShapeDtypeStruct((T, W), BF16),
        scratch_shapes=[pltpu.VMEM((S + POOL_HALO, W), F32)],
        compiler_params=_params(("arbitrary",), 40),
        name="pool_mixer",
    )(u, w_group.astype(BF16), scale.reshape(1, W))


SB_DEAD_LOG = -104.0


def _sb_kernel(q_ref, k_ref, v_ref, uu_ref, o_ref, *state, tq, n_heads, scale):
    acc_refs, run_refs = state[:n_heads], state[n_heads:]
    qi = pl.program_id(1)
    for ref in state:
        ref[...] = jnp.zeros_like(ref)
    uu = uu_ref[...]
    diag_strict = (lax.broadcasted_iota(jnp.int32, (tq, tq), 1)
                   < lax.broadcasted_iota(jnp.int32, (tq, tq), 0))

    def block(kb, on_diagonal):
        off = pl.multiple_of(kb * tq, tq)
        heads = [slice(h * HEAD_DIM, (h + 1) * HEAD_DIM) for h in range(n_heads)]
        zs = [lax.dot_general(q_ref[:, sl], k_ref[pl.ds(off, tq), sl], (((1,), (1,)), ((), ())),
                              preferred_element_type=F32) * scale for sl in heads]
        hits, sums = [], []
        for z in zs:
            soft = jnp.log(1.0 + jnp.exp(-jnp.abs(z)))
            log_fail = -jnp.maximum(z, 0.0) - soft
            hits.append(jnp.minimum(z, 0.0) - soft)
            if on_diagonal:
                log_fail = jnp.where(diag_strict, log_fail, 0.0)
            hi = log_fail.astype(BF16)
            lo = (log_fail - hi.astype(F32)).astype(BF16)
            sums.append(jnp.dot(jnp.concatenate([hi, lo], axis=1), uu, preferred_element_type=F32))
        worst = None
        for h, sl in enumerate(heads):
            run = run_refs[h][...]
            a = jnp.exp(hits[h] + sums[h][:, :tq] + run)
            if on_diagonal:
                a = jnp.where(diag_strict, a, 0.0)
            acc_refs[h][...] += jnp.dot(a.astype(BF16), v_ref[pl.ds(off, tq), sl],
                                      preferred_element_type=F32)
            run = run + sums[h][:, tq:]
            run_refs[h][...] = run
            worst = run if worst is None else jnp.maximum(worst, run)
        return jnp.max(worst)

    def more(c):
        it, worst = c
        return jnp.logical_and(it <= qi, worst > SB_DEAD_LOG)

    def step(c):
        it, _ = c
        return it + 1, block(qi - it, False)

    lax.while_loop(more, step, (jnp.int32(1), block(qi, True)))
    for h in range(n_heads):
        o_ref[:, h * HEAD_DIM:(h + 1) * HEAD_DIM] = acc_refs[h][...].astype(o_ref.dtype)


def sb_attention(qkv, B, S, n_heads, tq=128):
    T = qkv.shape[0]
    W = n_heads * HEAD_DIM
    assert S % tq == 0 and tq == HEAD_DIM
    nq = S // tq
    tri = (np.arange(tq)[:, None] > np.arange(tq)[None, :]).astype(np.float32)
    half = np.concatenate([tri, np.ones((tq, tq), np.float32)], axis=1)
    uu = jnp.asarray(np.concatenate([half, half], axis=0), BF16)
    return pl.pallas_call(
        functools.partial(_sb_kernel, tq=tq, n_heads=n_heads, scale=1.0 / float(np.sqrt(HEAD_DIM))),
        grid=(B, nq),
        in_specs=[
            pl.BlockSpec((tq, W), lambda b, i: (b * nq + i, 0)),
            pl.BlockSpec((S, W), lambda b, i: (b, 1)),
            pl.BlockSpec((S, W), lambda b, i: (b, 2)),
            pl.BlockSpec((2 * tq, 2 * tq), lambda b, i: (0, 0)),
        ],
        out_specs=pl.BlockSpec((tq, W), lambda b, i: (b * nq + i, 0)),
        out_shape=jax.ShapeDtypeStruct((T, W), BF16),
        scratch_shapes=[pltpu.VMEM((tq, HEAD_DIM), F32)] * (2 * n_heads),
        compiler_params=_params(("arbitrary", "arbitrary"), 40),
        name="sb_attention",
    )(qkv, qkv, qkv, uu)


def _xattn_kernel(q_ref, kv_ref, o_ref, *, n_heads, scale):
    W = n_heads * HEAD_DIM
    for h in range(n_heads):
        lo, hi = h * HEAD_DIM, (h + 1) * HEAD_DIM
        q = q_ref[:, lo:hi]
        k = kv_ref[:, lo:hi]
        v = kv_ref[:, W + lo:W + hi]
        s = lax.dot_general(q, k, (((1,), (1,)), ((), ())), preferred_element_type=F32) * scale
        m = jnp.max(s, axis=-1, keepdims=True)
        p = jnp.exp(s - m)
        p = p / jnp.sum(p, axis=-1, keepdims=True)
        o_ref[:, lo:hi] = jnp.dot(p.astype(BF16), v, preferred_element_type=F32).astype(o_ref.dtype)


def mem_attention(qkv, memkv, S, q_col_block, n_heads, tq=512):
    T = qkv.shape[0]
    W = n_heads * HEAD_DIM
    M = memkv.shape[0] // (T // S)
    assert S % tq == 0
    per = S // tq
    return pl.pallas_call(
        functools.partial(_xattn_kernel, n_heads=n_heads, scale=1.0 / float(np.sqrt(HEAD_DIM))),
        grid=(T // tq,),
        in_specs=[
            pl.BlockSpec((tq, W), lambda i: (i, q_col_block)),
            pl.BlockSpec((M, 2 * W), lambda i: (i // per, 0)),
        ],
        out_specs=pl.BlockSpec((tq, W), lambda i: (i, 0)),
        out_shape=jax.ShapeDtypeStruct((T, W), BF16),
        compiler_params=_params(("arbitrary",), 40),
        name="mem_attention",
    )(qkv, memkv)


def _merge_kernel(p_ref, sb_ref, xa_ref, g_ref, x_ref, wpp_ref, wsp_ref, wmp_ref, wout_ref,
                  gffn_ref, wr_ref, br_ref, h_ref, hn_ref, lg_ref):
    D = x_ref.shape[1]
    y_pool = jnp.dot(p_ref[...], wpp_ref[...], preferred_element_type=F32)
    y_sb = jnp.dot(sb_ref[...], wsp_ref[...], preferred_element_type=F32)
    y_xa = jnp.dot(xa_ref[...], wmp_ref[...], preferred_element_type=F32)
    merged = g_ref[:, 0:D] * y_pool + g_ref[:, D:2 * D] * y_sb + g_ref[:, 2 * D:3 * D] * y_xa
    h = x_ref[...] + jnp.dot(merged.astype(BF16), wout_ref[...], preferred_element_type=F32)
    h_ref[...] = h
    hn = _rms(h, gffn_ref[...])
    hn_ref[...] = hn
    lg_ref[...] = jnp.dot(hn.astype(BF16), wr_ref[...], preferred_element_type=F32) + br_ref[...]


def merge_block(p, a_sb, a_xa, gates, x, wpp, wsp, wmp, wout, g_ffn, w_router, b_router, tm=256):
    T, D = x.shape
    E = w_router.shape[1]
    wr = jnp.zeros((D, LANES), BF16).at[:, :E].set(w_router.astype(BF16))
    br = jnp.zeros((1, LANES), F32).at[0, :E].set(b_router)
    row = lambda w: pl.BlockSpec((tm, w), lambda i: (i, 0))
    full = lambda a: pl.BlockSpec(a.shape, lambda i: (0,) * a.ndim, pipeline_mode=pl.Buffered(1))
    weights = [wpp.astype(BF16), wsp.astype(BF16), wmp.astype(BF16), wout.astype(BF16),
               g_ffn.reshape(1, D), wr, br]
    return pl.pallas_call(
        _merge_kernel,
        grid=(T // tm,),
        in_specs=[row(p.shape[1]), row(a_sb.shape[1]), row(a_xa.shape[1]), row(3 * D), row(D)]
                 + [full(a) for a in weights],
        out_specs=[row(D), row(D), row(LANES)],
        out_shape=[jax.ShapeDtypeStruct((T, D), F32), jax.ShapeDtypeStruct((T, D), F32),
                   jax.ShapeDtypeStruct((T, LANES), F32)],
        compiler_params=_params(("arbitrary",), 56),
        name="merge_block",
    )(p, a_sb, a_xa, gates, x, *weights)


def _route_kernel(lg_ref, tri_ref, idx_ref, prob_ref, rank_ref, cnt_ref, carry_ref, *, n_experts):
    @pl.when(pl.program_id(0) == 0)
    def _():
        carry_ref[...] = jnp.zeros_like(carry_ref)

    tr = lg_ref.shape[0]
    lane = lax.broadcasted_iota(jnp.int32, (tr, LANES), 1)
    logit = jnp.where(lane < n_experts, lg_ref[...], -jnp.inf)
    onehot = jnp.zeros((tr, LANES), F32)
    vals, idxs = [], []
    for _ in range(TOP_K):
        m = jnp.max(logit, axis=-1, keepdims=True)
        ik = jnp.min(jnp.where(logit == m, lane, LANES), axis=-1, keepdims=True)
        sel = lane == ik
        onehot = jnp.where(sel, 1.0, onehot)
        logit = jnp.where(sel, -jnp.inf, logit)
        vals.append(m)
        idxs.append(ik)
    es = [jnp.exp(v - vals[0]) for v in vals]
    den = es[0] + es[1] + es[2] + es[3]
    before = jnp.dot(tri_ref[...], onehot.astype(BF16), preferred_element_type=F32) + carry_ref[...]
    carry_ref[...] = carry_ref[...] + jnp.sum(onehot, axis=0, keepdims=True)
    cnt_ref[...] = carry_ref[...]
    idx_out = jnp.zeros((tr, LANES), jnp.int32)
    prob_out = jnp.zeros((tr, LANES), F32)
    rank_out = jnp.zeros((tr, LANES), F32)
    for k in range(TOP_K):
        rk = jnp.sum(jnp.where(lane == idxs[k], before, 0.0), axis=-1, keepdims=True)
        idx_out = jnp.where(lane == k, idxs[k], idx_out)
        prob_out = jnp.where(lane == k, es[k] / den, prob_out)
        rank_out = jnp.where(lane == k, rk, rank_out)
    idx_ref[...] = idx_out
    prob_ref[...] = prob_out
    rank_ref[...] = rank_out.astype(jnp.int32)


def route(logits, n_experts, tr=256):
    T = logits.shape[0]
    tri = jnp.asarray((np.arange(tr)[None, :] < np.arange(tr)[:, None]).astype(np.float32), BF16)
    row = pl.BlockSpec((tr, LANES), lambda i: (i, 0))
    return pl.pallas_call(
        functools.partial(_route_kernel, n_experts=n_experts),
        grid=(T // tr,),
        in_specs=[row, pl.BlockSpec((tr, tr), lambda i: (0, 0))],
        out_specs=[row, row, row, pl.BlockSpec((1, LANES), lambda i: (0, 0))],
        out_shape=[jax.ShapeDtypeStruct((T, LANES), jnp.int32), jax.ShapeDtypeStruct((T, LANES), F32),
                   jax.ShapeDtypeStruct((T, LANES), jnp.int32), jax.ShapeDtypeStruct((1, LANES), F32)],
        scratch_shapes=[pltpu.VMEM((1, LANES), F32)],
        compiler_params=_params(("arbitrary",), 32),
        name="route",
    )(logits, tri)


def _gather_kernel(tok_ref, nu_ref, src_ref, o_ref, buf_ref, sem, *, tm):
    i = pl.program_id(0)

    @pl.when(i < nu_ref[0])
    def _():
        def issue(r, c):
            t = tok_ref[i * tm + r]
            pltpu.make_async_copy(src_ref.at[pl.ds(t, 1)], buf_ref.at[pl.ds(r, 1)], sem).start()
            return c

        lax.fori_loop(0, tm, issue, 0, unroll=8)
        pltpu.make_async_copy(src_ref.at[pl.ds(0, tm)], buf_ref, sem).wait()
        o_ref[...] = buf_ref[...].astype(o_ref.dtype)

    @pl.when(i >= nu_ref[0])
    def _():
        o_ref[...] = jnp.zeros_like(o_ref)


def gather_rows(rows_tok, n_used, src, tm):
    n_rows = rows_tok.shape[0]
    D = src.shape[1]
    return pl.pallas_call(
        functools.partial(_gather_kernel, tm=tm),
        grid_spec=pltpu.PrefetchScalarGridSpec(
            num_scalar_prefetch=2,
            grid=(n_rows // tm,),
            in_specs=[pl.BlockSpec(memory_space=pl.ANY)],
            out_specs=pl.BlockSpec((tm, D), lambda i, tok, nu: (i, 0)),
            scratch_shapes=[pltpu.VMEM((tm, D), F32), pltpu.SemaphoreType.DMA(())],
        ),
        out_shape=jax.ShapeDtypeStruct((n_rows, D), BF16),
        compiler_params=_params(("arbitrary",), 32, row_dma=True),
        name="gather_rows",
    )(rows_tok, n_used, src)


def _expert_changed(i, be_ref):
    return jnp.logical_or(i == 0, be_ref[i] != be_ref[jnp.maximum(i - 1, 0)])


def _moe_up_kernel(be_ref, nu_ref, x_ref, wg_ref, wu_ref, bg_ref, bu_ref, o_ref, wg_bf, wu_bf):
    i = pl.program_id(1)

    @pl.when(i < nu_ref[0])
    def _():
        @pl.when(_expert_changed(i, be_ref))
        def _():
            wg_bf[...] = wg_ref[0].astype(BF16)
            wu_bf[...] = wu_ref[0].astype(BF16)

        x = x_ref[...]
        gate = jnp.dot(x, wg_bf[...], preferred_element_type=F32) + bg_ref[0]
        up = jnp.dot(x, wu_bf[...], preferred_element_type=F32) + bu_ref[0]
        gate = jnp.minimum(gate, SWIGLU_LIMIT)
        up = jnp.clip(up, -SWIGLU_LIMIT, SWIGLU_LIMIT)
        o_ref[...] = (gate * jax.nn.sigmoid(SWIGLU_ALPHA * gate) * (up + 1.0)).astype(o_ref.dtype)

    @pl.when(i >= nu_ref[0])
    def _():
        o_ref[...] = jnp.zeros_like(o_ref)


def _moe_down_kernel(be_ref, nu_ref, h_ref, wd_ref, bd_ref, o_ref, wd_bf):
    i = pl.program_id(1)

    @pl.when(i < nu_ref[0])
    def _():
        @pl.when(_expert_changed(i, be_ref))
        def _():
            wd_bf[...] = wd_ref[0].astype(BF16)

        o_ref[...] = jnp.dot(h_ref[...], wd_bf[...], preferred_element_type=F32) + bd_ref[0]

    @pl.when(i >= nu_ref[0])
    def _():
        o_ref[...] = jnp.zeros_like(o_ref)


def _blk(i, nu):
    return jnp.minimum(i, nu[0] - 1)


def moe_up(blk_expert, n_used, xs, w_gate, b_gate, w_up, b_up, tm, tn=1024):
    n_rows, D = xs.shape
    E, _, F = w_gate.shape
    tn = min(tn, F)
    w_spec = pl.BlockSpec((1, D, tn), lambda j, i, be, nu: (be[_blk(i, nu)], 0, j))
    b_spec = pl.BlockSpec((1, 1, tn), lambda j, i, be, nu: (be[_blk(i, nu)], 0, j))
    return pl.pallas_call(
        _moe_up_kernel,
        grid_spec=pltpu.PrefetchScalarGridSpec(
            num_scalar_prefetch=2,
            grid=(F // tn, n_rows // tm),
            in_specs=[pl.BlockSpec((tm, D), lambda j, i, be, nu: (_blk(i, nu), 0)),
                      w_spec, w_spec, b_spec, b_spec],
            out_specs=pl.BlockSpec((tm, tn), lambda j, i, be, nu: (i, j)),
            scratch_shapes=[pltpu.VMEM((D, tn), BF16), pltpu.VMEM((D, tn), BF16)],
        ),
        out_shape=jax.ShapeDtypeStruct((n_rows, F), BF16),
        compiler_params=_params(("arbitrary", "arbitrary"), 56),
        name="moe_up",
    )(blk_expert, n_used, xs, w_gate, w_up, b_gate.reshape(E, 1, F), b_up.reshape(E, 1, F))


def moe_down(blk_expert, n_used, hs, w_down, b_down, tm, tn=2048):
    n_rows, F = hs.shape
    E, _, D = w_down.shape
    tn = min(tn, D)
    return pl.pallas_call(
        _moe_down_kernel,
        grid_spec=pltpu.PrefetchScalarGridSpec(
            num_scalar_prefetch=2,
            grid=(D // tn, n_rows // tm),
            in_specs=[pl.BlockSpec((tm, F), lambda j, i, be, nu: (_blk(i, nu), 0)),
                      pl.BlockSpec((1, F, tn), lambda j, i, be, nu: (be[_blk(i, nu)], 0, j)),
                      pl.BlockSpec((1, 1, tn), lambda j, i, be, nu: (be[_blk(i, nu)], 0, j))],
            out_specs=pl.BlockSpec((tm, tn), lambda j, i, be, nu: (i, j)),
            scratch_shapes=[pltpu.VMEM((F, tn), BF16)],
        ),
        out_shape=jax.ShapeDtypeStruct((n_rows, D), F32),
        compiler_params=_params(("arbitrary", "arbitrary"), 56),
        name="moe_down",
    )(blk_expert, n_used, hs, w_down, b_down.reshape(E, 1, D))


def _combine_kernel(dest_ref, y_ref, p_ref, h_ref, g_ref, o_ref, buf_ref, sem, *, tc, final_norm):
    i = pl.program_id(0)

    def issue(r, c):
        for k in range(TOP_K):
            d = dest_ref[(i * tc + r) * TOP_K + k]
            pltpu.make_async_copy(y_ref.at[pl.ds(d, 1)], buf_ref.at[k, pl.ds(r, 1)], sem).start()
        return c

    lax.fori_loop(0, tc, issue, 0, unroll=4)
    for k in range(TOP_K):
        pltpu.make_async_copy(y_ref.at[pl.ds(0, tc)], buf_ref.at[k], sem).wait()
    p = p_ref[...]
    h = h_ref[...]
    for k in range(TOP_K):
        h = h + p[:, k:k + 1] * buf_ref[k]
    o_ref[...] = _rms(h, g_ref[...]) if final_norm else h


def combine(dest_flat, y, probs, h, g_final, final_norm, tc=128):
    T, D = h.shape
    return pl.pallas_call(
        functools.partial(_combine_kernel, tc=tc, final_norm=final_norm),
        grid_spec=pltpu.PrefetchScalarGridSpec(
            num_scalar_prefetch=1,
            grid=(T // tc,),
            in_specs=[pl.BlockSpec(memory_space=pl.ANY),
                      pl.BlockSpec((tc, LANES), lambda i, d: (i, 0)),
                      pl.BlockSpec((tc, D), lambda i, d: (i, 0)),
                      pl.BlockSpec((1, D), lambda i, d: (0, 0))],
            out_specs=pl.BlockSpec((tc, D), lambda i, d: (i, 0)),
            scratch_shapes=[pltpu.VMEM((TOP_K, tc, D), F32), pltpu.SemaphoreType.DMA(())],
        ),
        out_shape=jax.ShapeDtypeStruct((T, D), F32),
        compiler_params=_params(("arbitrary",), 32, row_dma=True),
        name="combine",
    )(dest_flat, y, probs, h, g_final.reshape(1, D))


MOE_ROW_BLOCK = 256


def kernel(x, mem, norm_mix_g, norm_mem_g, w_in, w_pool_group, pool_scale, w_pool_proj, w_sb_proj,
           w_mem_kv, w_mem_proj, w_out, norm_ffn_g, w_router, b_router, w_gate, b_gate, w_up, b_up,
           w_down, b_down, norm_final_g):
    B, S, D = x.shape
    M = mem.shape[1]
    T = B * S
    depth = w_in.shape[0]
    pool_w = w_pool_proj.shape[1]
    sb_w = w_sb_proj.shape[1]
    xa_w = w_mem_proj.shape[1]
    E = w_router.shape[2]
    sb_heads = sb_w // HEAD_DIM
    xa_heads = xa_w // HEAD_DIM
    qkv_w = 3 * sb_w + xa_w
    assert (3 * sb_w) % xa_w == 0

    h = x.reshape(T, D)
    mem2 = mem.reshape(B * M, D)
    for l in range(depth):
        w_in_bf = w_in[l].astype(BF16)
        u_pool = norm_proj(h, norm_mix_g[l], w_in_bf[:, :pool_w], F32)
        qkv = norm_proj(h, norm_mix_g[l], w_in_bf[:, pool_w:pool_w + qkv_w], BF16, tn=qkv_w // 2)
        gates = norm_proj(h, norm_mix_g[l], w_in_bf[:, pool_w + qkv_w:], F32, act="sigmoid")
        memkv = norm_proj(mem2, norm_mem_g[l], w_mem_kv[l].astype(BF16), BF16)
        p = pool_mixer(u_pool, w_pool_group[l], pool_scale[l], B, S)
        a_sb = sb_attention(qkv, B, S, sb_heads)
        a_xa = mem_attention(qkv, memkv, S, (3 * sb_w) // xa_w, xa_heads)
        h, hn, logits = merge_block(p, a_sb, a_xa, gates, h, w_pool_proj[l], w_sb_proj[l],
                                    w_mem_proj[l], w_out[l], norm_ffn_g[l], w_router[l], b_router[l])
        idx, probs, rank, counts = route(logits, E)
        tm = MOE_ROW_BLOCK
        n_blk = (T * TOP_K) // tm + E
        counts = counts[0, :E].astype(jnp.int32)
        padded = (counts + tm - 1) // tm * tm
        pad_end = jnp.cumsum(padded)
        pad_start = pad_end - padded
        dest = pad_start[idx[:, :TOP_K]] + rank[:, :TOP_K]
        tok = jnp.broadcast_to(jnp.arange(T, dtype=jnp.int32)[:, None], (T, TOP_K))
        rows_tok = jnp.zeros((n_blk * tm,), jnp.int32).at[dest.reshape(-1)].set(tok.reshape(-1))
        blk_start = jnp.arange(n_blk, dtype=jnp.int32) * tm
        blk_expert = jnp.minimum(
            jnp.sum((pad_end[None, :] <= blk_start[:, None]).astype(jnp.int32), axis=1), E - 1)
        n_used = (pad_end[-1:] // tm).astype(jnp.int32)
        xs = gather_rows(rows_tok, n_used, hn, tm)
        hs = moe_up(blk_expert, n_used, xs, w_gate[l], b_gate[l], w_up[l], b_up[l], tm)
        ys = moe_down(blk_expert, n_used, hs, w_down[l], b_down[l], tm)
        h = combine(dest.reshape(-1).astype(jnp.int32), ys, probs, h, norm_final_g,
                    final_norm=(l == depth - 1))
    return h.reshape(B, S, D)
```

```python
import functools

import numpy as np
import jax
import jax.numpy as jnp
from jax import lax
from jax.experimental import pallas as pl
from jax.experimental.pallas import tpu as pltpu

EPS = 1e-5
HEAD_DIM = 128
POOL_WINDOWS = (2, 4, 8, 16)
POOL_HALO = 16
TOP_K = 4
SWIGLU_LIMIT = 7.0
SWIGLU_ALPHA = 1.702
LANES = 128

BF16 = jnp.bfloat16
F32 = jnp.float32

MIB = 1024 * 1024


def _params(sem, vmem_mib, row_dma=False):
    return pltpu.CompilerParams(dimension_semantics=sem, vmem_limit_bytes=vmem_mib * MIB,
                                disable_bounds_checks=row_dma)


def _rms(x, g):
    ms = jnp.mean(x * x, axis=-1, keepdims=True)
    return x * lax.rsqrt(ms + EPS) * g


def _norm_proj_kernel(x_ref, g_ref, w_ref, o_ref, xn_ref, *, act):
    @pl.when(pl.program_id(1) == 0)
    def _():
        xn_ref[...] = _rms(x_ref[...], g_ref[...]).astype(BF16)

    y = jnp.dot(xn_ref[...], w_ref[...], preferred_element_type=F32)
    if act == "sigmoid":
        y = jax.nn.sigmoid(y)
    o_ref[...] = y.astype(o_ref.dtype)


def norm_proj(x, g, w, out_dtype, act=None, tm=1024, tn=1024):
    R, D = x.shape
    N = w.shape[1]
    tm = min(tm, R)
    tn = min(tn, N)
    assert R % tm == 0 and N % tn == 0
    return pl.pallas_call(
        functools.partial(_norm_proj_kernel, act=act),
        grid=(R // tm, N // tn),
        in_specs=[
            pl.BlockSpec((tm, D), lambda i, j: (i, 0)),
            pl.BlockSpec((1, D), lambda i, j: (0, 0)),
            pl.BlockSpec((D, tn), lambda i, j: (0, j)),
        ],
        out_specs=pl.BlockSpec((tm, tn), lambda i, j: (i, j)),
        out_shape=jax.ShapeDtypeStruct((R, N), out_dtype),
        scratch_shapes=[pltpu.VMEM((tm, D), BF16)],
        compiler_params=_params(("arbitrary", "arbitrary"), 48),
        name="norm_proj",
    )(x, g.reshape(1, D), w)


def _pool_kernel(u_ref, wg_ref, sc_ref, o_ref, pad_ref, *, S):
    W = u_ref.shape[1]
    pad_ref[0:POOL_HALO, :] = jnp.zeros((POOL_HALO, W), F32)
    pad_ref[POOL_HALO:POOL_HALO + S, :] = u_ref[...]
    t = lax.broadcasted_iota(jnp.int32, (S, 1), 0).astype(F32)
    for g, w in enumerate(POOL_WINDOWS):
        lo, hi = g * HEAD_DIM, (g + 1) * HEAD_DIM
        u = pad_ref[POOL_HALO:POOL_HALO + S, lo:hi]
        acc = u
        for k in range(1, w):
            acc = acc + pad_ref[POOL_HALO - k:POOL_HALO - k + S, lo:hi]
        mean = acc / jnp.minimum(t + 1.0, float(w))
        d = (mean - u).astype(BF16)
        y = jnp.dot(d, wg_ref[g], preferred_element_type=F32) * sc_ref[:, lo:hi]
        o_ref[:, lo:hi] = y.astype(o_ref.dtype)


def pool_mixer(u, w_group, scale, B, S):
    T, W = u.shape
    G = len(POOL_WINDOWS)
    assert W == G * HEAD_DIM
    return pl.pallas_call(
        functools.partial(_pool_kernel, S=S),
        grid=(B,),
        in_specs=[
            pl.BlockSpec((S, W), lambda b: (b, 0)),
            pl.BlockSpec((G, HEAD_DIM, HEAD_DIM), lambda b: (0, 0, 0)),
            pl.BlockSpec((1, W), lambda b: (0, 0)),
        ],
        out_specs=pl.BlockSpec((S, W), lambda b: (b, 0)),
        out_shape=jax.ShapeDtypeStruct((T, W), BF16),
        scratch_shapes=[pltpu.VMEM((S + POOL_HALO, W), F32)],
        compiler_params=_params(("arbitrary",), 40),
        name="pool_mixer",
    )(u, w_group.astype(BF16), scale.reshape(1, W))


SB_DEAD_LOG = -104.0


def _sb_kernel(q_ref, k_ref, v_ref, uu_ref, o_ref, *state, tq, n_heads, scale):
    acc_refs, run_refs = state[:n_heads], state[n_heads:]
    qi = pl.program_id(1)
    for ref in state:
        ref[...] = jnp.zeros_like(ref)
    uu = uu_ref[...]
    diag_strict = (lax.broadcasted_iota(jnp.int32, (tq, tq), 1)
                   < lax.broadcasted_iota(jnp.int32, (tq, tq), 0))

    def block(kb, on_diagonal):
        off = pl.multiple_of(kb * tq, tq)
        heads = [slice(h * HEAD_DIM, (h + 1) * HEAD_DIM) for h in range(n_heads)]
        zs = [lax.dot_general(q_ref[:, sl], k_ref[pl.ds(off, tq), sl], (((1,), (1,)), ((), ())),
                              preferred_element_type=F32) * scale for sl in heads]
        hits, sums = [], []
        for z in zs:
            soft = jnp.log(1.0 + jnp.exp(-jnp.abs(z)))
            log_fail = -jnp.maximum(z, 0.0) - soft
            hits.append(jnp.minimum(z, 0.0) - soft)
            if on_diagonal:
                log_fail = jnp.where(diag_strict, log_fail, 0.0)
            hi = log_fail.astype(BF16)
            lo = (log_fail - hi.astype(F32)).astype(BF16)
            sums.append(jnp.dot(jnp.concatenate([hi, lo], axis=1), uu, preferred_element_type=F32))
        worst = None
        for h, sl in enumerate(heads):
            run = run_refs[h][...]
            a = jnp.exp(hits[h] + sums[h][:, :tq] + run)
            if on_diagonal:
                a = jnp.where(diag_strict, a, 0.0)
            acc_refs[h][...] += jnp.dot(a.astype(BF16), v_ref[pl.ds(off, tq), sl],
                                      preferred_element_type=F32)
            run = run + sums[h][:, tq:]
            run_refs[h][...] = run
            worst = run if worst is None else jnp.maximum(worst, run)
        return jnp.max(worst)

    def more(c):
        it, worst = c
        return jnp.logical_and(it <= qi, worst > SB_DEAD_LOG)

    def step(c):
        it, _ = c
        return it + 1, block(qi - it, False)

    lax.while_loop(more, step, (jnp.int32(1), block(qi, True)))
    for h in range(n_heads):
        o_ref[:, h * HEAD_DIM:(h + 1) * HEAD_DIM] = acc_refs[h][...].astype(o_ref.dtype)


def sb_attention(qkv, B, S, n_heads, tq=128):
    T = qkv.shape[0]
    W = n_heads * HEAD_DIM
    assert S % tq == 0 and tq == HEAD_DIM
    nq = S // tq
    tri = (np.arange(tq)[:, None] > np.arange(tq)[None, :]).astype(np.float32)
    half = np.concatenate([tri, np.ones((tq, tq), np.float32)], axis=1)
    uu = jnp.asarray(np.concatenate([half, half], axis=0), BF16)
    return pl.pallas_call(
        functools.partial(_sb_kernel, tq=tq, n_heads=n_heads, scale=1.0 / float(np.sqrt(HEAD_DIM))),
        grid=(B, nq),
        in_specs=[
            pl.BlockSpec((tq, W), lambda b, i: (b * nq + i, 0)),
            pl.BlockSpec((S, W), lambda b, i: (b, 1)),
            pl.BlockSpec((S, W), lambda b, i: (b, 2)),
            pl.BlockSpec((2 * tq, 2 * tq), lambda b, i: (0, 0)),
        ],
        out_specs=pl.BlockSpec((tq, W), lambda b, i: (b * nq + i, 0)),
        out_shape=jax.ShapeDtypeStruct((T, W), BF16),
        scratch_shapes=[pltpu.VMEM((tq, HEAD_DIM), F32)] * (2 * n_heads),
        compiler_params=_params(("arbitrary", "arbitrary"), 40),
        name="sb_attention",
    )(qkv, qkv, qkv, uu)


def _xattn_kernel(q_ref, kv_ref, o_ref, *, n_heads, scale):
    W = n_heads * HEAD_DIM
    for h in range(n_heads):
        lo, hi = h * HEAD_DIM, (h + 1) * HEAD_DIM
        q = q_ref[:, lo:hi]
        k = kv_ref[:, lo:hi]
        v = kv_ref[:, W + lo:W + hi]
        s = lax.dot_general(q, k, (((1,), (1,)), ((), ())), preferred_element_type=F32) * scale
        m = jnp.max(s, axis=-1, keepdims=True)
        p = jnp.exp(s - m)
        p = p / jnp.sum(p, axis=-1, keepdims=True)
        o_ref[:, lo:hi] = jnp.dot(p.astype(BF16), v, preferred_element_type=F32).astype(o_ref.dtype)


def mem_attention(qkv, memkv, S, q_col_block, n_heads, tq=512):
    T = qkv.shape[0]
    W = n_heads * HEAD_DIM
    M = memkv.shape[0] // (T // S)
    assert S % tq == 0
    per = S // tq
    return pl.pallas_call(
        functools.partial(_xattn_kernel, n_heads=n_heads, scale=1.0 / float(np.sqrt(HEAD_DIM))),
        grid=(T // tq,),
        in_specs=[
            pl.BlockSpec((tq, W), lambda i: (i, q_col_block)),
            pl.BlockSpec((M, 2 * W), lambda i: (i // per, 0)),
        ],
        out_specs=pl.BlockSpec((tq, W), lambda i: (i, 0)),
        out_shape=jax.ShapeDtypeStruct((T, W), BF16),
        compiler_params=_params(("arbitrary",), 40),
        name="mem_attention",
    )(qkv, memkv)


def _merge_kernel(p_ref, sb_ref, xa_ref, g_ref, x_ref, wpp_ref, wsp_ref, wmp_ref, wout_ref,
                  gffn_ref, wr_ref, br_ref, h_ref, hn_ref, lg_ref):
    D = x_ref.shape[1]
    y_pool = jnp.dot(p_ref[...], wpp_ref[...], preferred_element_type=F32)
    y_sb = jnp.dot(sb_ref[...], wsp_ref[...], preferred_element_type=F32)
    y_xa = jnp.dot(xa_ref[...], wmp_ref[...], preferred_element_type=F32)
    merged = g_ref[:, 0:D] * y_pool + g_ref[:, D:2 * D] * y_sb + g_ref[:, 2 * D:3 * D] * y_xa
    h = x_ref[...] + jnp.dot(merged.astype(BF16), wout_ref[...], preferred_element_type=F32)
    h_ref[...] = h
    hn = _rms(h, gffn_ref[...])
    hn_ref[...] = hn
    lg_ref[...] = jnp.dot(hn.astype(BF16), wr_ref[...], preferred_element_type=F32) + br_ref[...]


def merge_block(p, a_sb, a_xa, gates, x, wpp, wsp, wmp, wout, g_ffn, w_router, b_router, tm=256):
    T, D = x.shape
    E = w_router.shape[1]
    wr = jnp.zeros((D, LANES), BF16).at[:, :E].set(w_router.astype(BF16))
    br = jnp.zeros((1, LANES), F32).at[0, :E].set(b_router)
    row = lambda w: pl.BlockSpec((tm, w), lambda i: (i, 0))
    full = lambda a: pl.BlockSpec(a.shape, lambda i: (0,) * a.ndim, pipeline_mode=pl.Buffered(1))
    weights = [wpp.astype(BF16), wsp.astype(BF16), wmp.astype(BF16), wout.astype(BF16),
               g_ffn.reshape(1, D), wr, br]
    return pl.pallas_call(
        _merge_kernel,
        grid=(T // tm,),
        in_specs=[row(p.shape[1]), row(a_sb.shape[1]), row(a_xa.shape[1]), row(3 * D), row(D)]
                 + [full(a) for a in weights],
        out_specs=[row(D), row(D), row(LANES)],
        out_shape=[jax.ShapeDtypeStruct((T, D), F32), jax.ShapeDtypeStruct((T, D), F32),
                   jax.ShapeDtypeStruct((T, LANES), F32)],
        compiler_params=_params(("arbitrary",), 56),
        name="merge_block",
    )(p, a_sb, a_xa, gates, x, *weights)


def _route_kernel(lg_ref, tri_ref, idx_ref, prob_ref, rank_ref, cnt_ref, carry_ref, *, n_experts):
    @pl.when(pl.program_id(0) == 0)
    def _():
        carry_ref[...] = jnp.zeros_like(carry_ref)

    tr = lg_ref.shape[0]
    lane = lax.broadcasted_iota(jnp.int32, (tr, LANES), 1)
    logit = jnp.where(lane < n_experts, lg_ref[...], -jnp.inf)
    onehot = jnp.zeros((tr, LANES), F32)
    vals, idxs = [], []
    for _ in range(TOP_K):
        m = jnp.max(logit, axis=-1, keepdims=True)
        ik = jnp.min(jnp.where(logit == m, lane, LANES), axis=-1, keepdims=True)
        sel = lane == ik
        onehot = jnp.where(sel, 1.0, onehot)
        logit = jnp.where(sel, -jnp.inf, logit)
        vals.append(m)
        idxs.append(ik)
    es = [jnp.exp(v - vals[0]) for v in vals]
    den = es[0] + es[1] + es[2] + es[3]
    before = jnp.dot(tri_ref[...], onehot.astype(BF16), preferred_element_type=F32) + carry_ref[...]
    carry_ref[...] = carry_ref[...] + jnp.sum(onehot, axis=0, keepdims=True)
    cnt_ref[...] = carry_ref[...]
    idx_out = jnp.zeros((tr, LANES), jnp.int32)
    prob_out = jnp.zeros((tr, LANES), F32)
    rank_out = jnp.zeros((tr, LANES), F32)
    for k in range(TOP_K):
        rk = jnp.sum(jnp.where(lane == idxs[k], before, 0.0), axis=-1, keepdims=True)
        idx_out = jnp.where(lane == k, idxs[k], idx_out)
        prob_out = jnp.where(lane == k, es[k] / den, prob_out)
        rank_out = jnp.where(lane == k, rk, rank_out)
    idx_ref[...] = idx_out
    prob_ref[...] = prob_out
    rank_ref[...] = rank_out.astype(jnp.int32)


def route(logits, n_experts, tr=256):
    T = logits.shape[0]
    tri = jnp.asarray((np.arange(tr)[None, :] < np.arange(tr)[:, None]).astype(np.float32), BF16)
    row = pl.BlockSpec((tr, LANES), lambda i: (i, 0))
    return pl.pallas_call(
        functools.partial(_route_kernel, n_experts=n_experts),
        grid=(T // tr,),
        in_specs=[row, pl.BlockSpec((tr, tr), lambda i: (0, 0))],
        out_specs=[row, row, row, pl.BlockSpec((1, LANES), lambda i: (0, 0))],
        out_shape=[jax.ShapeDtypeStruct((T, LANES), jnp.int32), jax.ShapeDtypeStruct((T, LANES), F32),
                   jax.ShapeDtypeStruct((T, LANES), jnp.int32), jax.ShapeDtypeStruct((1, LANES), F32)],
        scratch_shapes=[pltpu.VMEM((1, LANES), F32)],
        compiler_params=_params(("arbitrary",), 32),
        name="route",
    )(logits, tri)


INVERT_STEPS = 16


def _invert_kernel(dest_ref, o_ref):
    phase, s = pl.program_id(0), pl.program_id(1)
    rows_per_step = o_ref.shape[0] // INVERT_STEPS
    toks_per_step = dest_ref.shape[0] // TOP_K // INVERT_STEPS

    @pl.when(phase == 0)
    def _():
        def clear(r, c):
            o_ref[s * rows_per_step + r] = 0
            return c

        lax.fori_loop(0, rows_per_step, clear, 0, unroll=8)

    @pl.when(phase == 1)
    def _():
        def put(j, c):
            t = s * toks_per_step + j
            for k in range(TOP_K):
                o_ref[dest_ref[t * TOP_K + k]] = t
            return c

        lax.fori_loop(0, toks_per_step, put, 0, unroll=4)


def invert_routing(dest_flat, n_rows):
    assert n_rows % (8 * INVERT_STEPS) == 0 and dest_flat.shape[0] % (4 * TOP_K * INVERT_STEPS) == 0
    return pl.pallas_call(
        _invert_kernel,
        grid=(2, INVERT_STEPS),
        in_specs=[pl.BlockSpec(memory_space=pltpu.SMEM)],
        out_specs=pl.BlockSpec(memory_space=pltpu.SMEM),
        out_shape=jax.ShapeDtypeStruct((n_rows,), jnp.int32),
        compiler_params=pltpu.CompilerParams(dimension_semantics=("arbitrary", "arbitrary")),
        name="invert_routing",
    )(dest_flat)


def _expert_changed(i, be_ref):
    return jnp.logical_or(i == 0, be_ref[i] != be_ref[jnp.maximum(i - 1, 0)])


def _moe_up_kernel(be_ref, nu_ref, x_ref, wg_ref, wu_ref, bg_ref, bu_ref, o_ref, wg_bf, wu_bf):
    i = pl.program_id(1)

    @pl.when(i < nu_ref[0])
    def _():
        @pl.when(_expert_changed(i, be_ref))
        def _():
            wg_bf[...] = wg_ref[0].astype(BF16)
            wu_bf[...] = wu_ref[0].astype(BF16)

        o_ref[...] = _swiglu(x_ref[...], wg_bf, wu_bf, bg_ref, bu_ref).astype(o_ref.dtype)

    @pl.when(i >= nu_ref[0])
    def _():
        o_ref[...] = jnp.zeros_like(o_ref)


def _swiglu(x, wg_bf, wu_bf, bg_ref, bu_ref):
    gate = jnp.dot(x, wg_bf[...], preferred_element_type=F32) + bg_ref[0]
    up = jnp.dot(x, wu_bf[...], preferred_element_type=F32) + bu_ref[0]
    gate = jnp.minimum(gate, SWIGLU_LIMIT)
    up = jnp.clip(up, -SWIGLU_LIMIT, SWIGLU_LIMIT)
    return gate * jax.nn.sigmoid(SWIGLU_ALPHA * gate) * (up + 1.0)


def _moe_up_gather_kernel(tok_ref, be_ref, nu_ref, src_ref, wg_ref, wu_ref, bg_ref, bu_ref,
                          o_ref, xs_ref, wg_bf, wu_bf, buf0, buf1, sem0, sem1, *, tm):
    i = pl.program_id(0)
    nu = nu_ref[0]

    def issue(blk, buf, sem):
        for r in range(tm):
            t = tok_ref[blk * tm + r]
            pltpu.make_async_copy(src_ref.at[pl.ds(t, 1)], buf.at[pl.ds(r, 1)], sem).start()

    def wait(buf, sem):
        pltpu.make_async_copy(src_ref.at[pl.ds(0, tm)], buf, sem).wait()

    @pl.when(i == 0)
    def _():
        issue(0, buf0, sem0)

    def work(buf, sem, next_buf, next_sem):
        @pl.when(_expert_changed(i, be_ref))
        def _():
            wg_bf[...] = wg_ref[0].astype(BF16)
            wu_bf[...] = wu_ref[0].astype(BF16)

        wait(buf, sem)
        x = buf[...].astype(BF16)
        xs_ref[...] = x
        o_ref[...] = _swiglu(x, wg_bf, wu_bf, bg_ref, bu_ref).astype(o_ref.dtype)
        issue(i + 1, next_buf, next_sem)

    @pl.when(jnp.logical_and(i < nu, i % 2 == 0))
    def _():
        work(buf0, sem0, buf1, sem1)

    @pl.when(jnp.logical_and(i < nu, i % 2 == 1))
    def _():
        work(buf1, sem1, buf0, sem0)

    @pl.when(i >= nu)
    def _():
        o_ref[...] = jnp.zeros_like(o_ref)
        xs_ref[...] = jnp.zeros_like(xs_ref)

    @pl.when(jnp.logical_and(i == nu, nu % 2 == 0))
    def _():
        wait(buf0, sem0)

    @pl.when(jnp.logical_and(i == nu, nu % 2 == 1))
    def _():
        wait(buf1, sem1)


def moe_up_gather(rows_tok, blk_expert, n_used, src, w_gate, b_gate, w_up, b_up, tm, tn=1024):
    n_rows = rows_tok.shape[0]
    D = src.shape[1]
    E, _, F = w_gate.shape
    w_spec = pl.BlockSpec((1, D, tn), lambda i, tok, be, nu: (be[_blk(i, nu)], 0, 0))
    b_spec = pl.BlockSpec((1, 1, tn), lambda i, tok, be, nu: (be[_blk(i, nu)], 0, 0))
    return pl.pallas_call(
        functools.partial(_moe_up_gather_kernel, tm=tm),
        grid_spec=pltpu.PrefetchScalarGridSpec(
            num_scalar_prefetch=3,
            grid=(n_rows // tm,),
            in_specs=[pl.BlockSpec(memory_space=pl.ANY), w_spec, w_spec, b_spec, b_spec],
            out_specs=[pl.BlockSpec((tm, tn), lambda i, tok, be, nu: (i, 0)),
                       pl.BlockSpec((tm, D), lambda i, tok, be, nu: (i, 0))],
            scratch_shapes=[pltpu.VMEM((D, tn), BF16), pltpu.VMEM((D, tn), BF16),
                            pltpu.VMEM((tm, D), F32), pltpu.VMEM((tm, D), F32),
                            pltpu.SemaphoreType.DMA(()), pltpu.SemaphoreType.DMA(())],
        ),
        out_shape=[jax.ShapeDtypeStruct((n_rows, tn), BF16), jax.ShapeDtypeStruct((n_rows, D), BF16)],
        compiler_params=_params(("arbitrary",), 56, row_dma=True),
        name="moe_up_gather",
    )(rows_tok, blk_expert, n_used, src, w_gate, w_up, b_gate.reshape(E, 1, F), b_up.reshape(E, 1, F))


def _moe_down_kernel(be_ref, nu_ref, h0_ref, h1_ref, wd_ref, bd_ref, o_ref, wd_bf):
    i = pl.program_id(1)
    F0 = h0_ref.shape[1]

    @pl.when(i < nu_ref[0])
    def _():
        @pl.when(_expert_changed(i, be_ref))
        def _():
            wd_bf[...] = wd_ref[0].astype(BF16)

        o_ref[...] = (jnp.dot(h0_ref[...], wd_bf[:F0, :], preferred_element_type=F32)
                      + jnp.dot(h1_ref[...], wd_bf[F0:, :], preferred_element_type=F32) + bd_ref[0])

    @pl.when(i >= nu_ref[0])
    def _():
        o_ref[...] = jnp.zeros_like(o_ref)


def _blk(i, nu):
    return jnp.minimum(i, nu[0] - 1)


def moe_up(blk_expert, n_used, xs, w_gate, b_gate, w_up, b_up, tm, tn=1024):
    n_rows, D = xs.shape
    E, _, F = w_gate.shape
    w_spec = pl.BlockSpec((1, D, tn), lambda j, i, be, nu: (be[_blk(i, nu)], 0, j + 1))
    b_spec = pl.BlockSpec((1, 1, tn), lambda j, i, be, nu: (be[_blk(i, nu)], 0, j + 1))
    return pl.pallas_call(
        _moe_up_kernel,
        grid_spec=pltpu.PrefetchScalarGridSpec(
            num_scalar_prefetch=2,
            grid=(F // tn - 1, n_rows // tm),
            in_specs=[pl.BlockSpec((tm, D), lambda j, i, be, nu: (_blk(i, nu), 0)),
                      w_spec, w_spec, b_spec, b_spec],
            out_specs=pl.BlockSpec((tm, tn), lambda j, i, be, nu: (i, j)),
            scratch_shapes=[pltpu.VMEM((D, tn), BF16), pltpu.VMEM((D, tn), BF16)],
        ),
        out_shape=jax.ShapeDtypeStruct((n_rows, F - tn), BF16),
        compiler_params=_params(("arbitrary", "arbitrary"), 56),
        name="moe_up",
    )(blk_expert, n_used, xs, w_gate, w_up, b_gate.reshape(E, 1, F), b_up.reshape(E, 1, F))


def moe_down(blk_expert, n_used, hs0, hs1, w_down, b_down, tm, tn=2048):
    n_rows, F0 = hs0.shape
    F1 = hs1.shape[1]
    F = F0 + F1
    E, _, D = w_down.shape
    tn = min(tn, D)
    return pl.pallas_call(
        _moe_down_kernel,
        grid_spec=pltpu.PrefetchScalarGridSpec(
            num_scalar_prefetch=2,
            grid=(D // tn, n_rows // tm),
            in_specs=[pl.BlockSpec((tm, F0), lambda j, i, be, nu: (_blk(i, nu), 0)),
                      pl.BlockSpec((tm, F1), lambda j, i, be, nu: (_blk(i, nu), 0)),
                      pl.BlockSpec((1, F, tn), lambda j, i, be, nu: (be[_blk(i, nu)], 0, j)),
                      pl.BlockSpec((1, 1, tn), lambda j, i, be, nu: (be[_blk(i, nu)], 0, j))],
            out_specs=pl.BlockSpec((tm, tn), lambda j, i, be, nu: (i, j)),
            scratch_shapes=[pltpu.VMEM((F, tn), BF16)],
        ),
        out_shape=jax.ShapeDtypeStruct((n_rows, D), F32),
        compiler_params=_params(("arbitrary", "arbitrary"), 56),
        name="moe_down",
    )(blk_expert, n_used, hs0, hs1, w_down, b_down.reshape(E, 1, D))


def _combine_kernel(dest_ref, y_ref, p_ref, h_ref, g_ref, o_ref, buf0, buf1, sem0, sem1, *,
                    tc, final_norm):
    i = pl.program_id(0)

    def issue(blk, buf, sem):
        def rows(r, c):
            for k in range(TOP_K):
                d = dest_ref[(blk * tc + r) * TOP_K + k]
                pltpu.make_async_copy(y_ref.at[pl.ds(d, 1)], buf.at[k, pl.ds(r, 1)], sem).start()
            return c

        lax.fori_loop(0, tc, rows, 0, unroll=4)

    @pl.when(i == 0)
    def _():
        issue(0, buf0, sem0)

    def work(buf, sem, next_buf, next_sem):
        @pl.when(i + 1 < pl.num_programs(0))
        def _():
            issue(i + 1, next_buf, next_sem)

        for k in range(TOP_K):
            pltpu.make_async_copy(y_ref.at[pl.ds(0, tc)], buf.at[k], sem).wait()
        p = p_ref[...]
        h = h_ref[...]
        for k in range(TOP_K):
            h = h + p[:, k:k + 1] * buf[k]
        o_ref[...] = _rms(h, g_ref[...]) if final_norm else h

    @pl.when(i % 2 == 0)
    def _():
        work(buf0, sem0, buf1, sem1)

    @pl.when(i % 2 == 1)
    def _():
        work(buf1, sem1, buf0, sem0)


def combine(dest_flat, y, probs, h, g_final, final_norm, tc=128):
    T, D = h.shape
    return pl.pallas_call(
        functools.partial(_combine_kernel, tc=tc, final_norm=final_norm),
        grid_spec=pltpu.PrefetchScalarGridSpec(
            num_scalar_prefetch=1,
            grid=(T // tc,),
            in_specs=[pl.BlockSpec(memory_space=pl.ANY),
                      pl.BlockSpec((tc, LANES), lambda i, d: (i, 0)),
                      pl.BlockSpec((tc, D), lambda i, d: (i, 0)),
                      pl.BlockSpec((1, D), lambda i, d: (0, 0))],
            out_specs=pl.BlockSpec((tc, D), lambda i, d: (i, 0)),
            scratch_shapes=[pltpu.VMEM((TOP_K, tc, D), F32), pltpu.VMEM((TOP_K, tc, D), F32),
                            pltpu.SemaphoreType.DMA(()), pltpu.SemaphoreType.DMA(())],
        ),
        out_shape=jax.ShapeDtypeStruct((T, D), F32),
        compiler_params=_params(("arbitrary",), 32, row_dma=True),
        name="combine",
    )(dest_flat, y, probs, h, g_final.reshape(1, D))


MOE_ROW_BLOCK = 256


def kernel(x, mem, norm_mix_g, norm_mem_g, w_in, w_pool_group, pool_scale, w_pool_proj, w_sb_proj,
           w_mem_kv, w_mem_proj, w_out, norm_ffn_g, w_router, b_router, w_gate, b_gate, w_up, b_up,
           w_down, b_down, norm_final_g):
    B, S, D = x.shape
    M = mem.shape[1]
    T = B * S
    depth = w_in.shape[0]
    pool_w = w_pool_proj.shape[1]
    sb_w = w_sb_proj.shape[1]
    xa_w = w_mem_proj.shape[1]
    E = w_router.shape[2]
    sb_heads = sb_w // HEAD_DIM
    xa_heads = xa_w // HEAD_DIM
    qkv_w = 3 * sb_w + xa_w
    assert (3 * sb_w) % xa_w == 0

    h = x.reshape(T, D)
    mem2 = mem.reshape(B * M, D)
    for l in range(depth):
        w_in_bf = w_in[l].astype(BF16)
        u_pool = norm_proj(h, norm_mix_g[l], w_in_bf[:, :pool_w], F32)
        qkv = norm_proj(h, norm_mix_g[l], w_in_bf[:, pool_w:pool_w + qkv_w], BF16, tn=qkv_w // 2)
        gates = norm_proj(h, norm_mix_g[l], w_in_bf[:, pool_w + qkv_w:], F32, act="sigmoid")
        memkv = norm_proj(mem2, norm_mem_g[l], w_mem_kv[l].astype(BF16), BF16)
        p = pool_mixer(u_pool, w_pool_group[l], pool_scale[l], B, S)
        a_sb = sb_attention(qkv, B, S, sb_heads)
        a_xa = mem_attention(qkv, memkv, S, (3 * sb_w) // xa_w, xa_heads)
        h, hn, logits = merge_block(p, a_sb, a_xa, gates, h, w_pool_proj[l], w_sb_proj[l],
                                    w_mem_proj[l], w_out[l], norm_ffn_g[l], w_router[l], b_router[l])
        idx, probs, rank, counts = route(logits, E)
        tm = MOE_ROW_BLOCK
        n_blk = (T * TOP_K) // tm + E + 1
        counts = counts[0, :E].astype(jnp.int32)
        padded = (counts + tm - 1) // tm * tm
        pad_end = jnp.cumsum(padded)
        pad_start = pad_end - padded
        dest = (pad_start[idx[:, :TOP_K]] + rank[:, :TOP_K]).reshape(-1)
        rows_tok = invert_routing(dest, n_blk * tm)
        blk_start = jnp.arange(n_blk, dtype=jnp.int32) * tm
        blk_expert = jnp.minimum(
            jnp.sum((pad_end[None, :] <= blk_start[:, None]).astype(jnp.int32), axis=1), E - 1)
        n_used = (pad_end[-1:] // tm).astype(jnp.int32)
        hs0, xs = moe_up_gather(rows_tok, blk_expert, n_used, hn, w_gate[l], b_gate[l], w_up[l],
                                b_up[l], tm)
        hs1 = moe_up(blk_expert, n_used, xs, w_gate[l], b_gate[l], w_up[l], b_up[l], tm)
        ys = moe_down(blk_expert, n_used, hs0, hs1, w_down[l], b_down[l], tm)
        h = combine(dest, ys, probs, h, norm_final_g,
                    final_norm=(l == depth - 1))
    return h.reshape(B, S, D)
```

```python
import functools

import numpy as np
import jax
import jax.numpy as jnp
from jax import lax
from jax.experimental import pallas as pl
from jax.experimental.pallas import tpu as pltpu

EPS = 1e-5
HEAD_DIM = 128
POOL_WINDOWS = (2, 4, 8, 16)
POOL_HALO = 16
TOP_K = 4
SWIGLU_LIMIT = 7.0
SWIGLU_ALPHA = 1.702
LANES = 128

BF16 = jnp.bfloat16
F32 = jnp.float32

MIB = 1024 * 1024


def _params(sem, vmem_mib, row_dma=False):
    return pltpu.CompilerParams(dimension_semantics=sem, vmem_limit_bytes=vmem_mib * MIB,
                                disable_bounds_checks=row_dma)


def _rms(x, g):
    ms = jnp.mean(x * x, axis=-1, keepdims=True)
    return x * lax.rsqrt(ms + EPS) * g


def _norm_proj_kernel(x_ref, g_ref, w_ref, o_ref, xn_ref, *, act):
    @pl.when(pl.program_id(1) == 0)
    def _():
        xn_ref[...] = _rms(x_ref[...], g_ref[...]).astype(BF16)

    y = jnp.dot(xn_ref[...], w_ref[...], preferred_element_type=F32)
    if act == "sigmoid":
        y = jax.nn.sigmoid(y)
    o_ref[...] = y.astype(o_ref.dtype)


def norm_proj(x, g, w, out_dtype, act=None, tm=1024, tn=1024):
    R, D = x.shape
    N = w.shape[1]
    tm = min(tm, R)
    tn = min(tn, N)
    assert R % tm == 0 and N % tn == 0
    return pl.pallas_call(
        functools.partial(_norm_proj_kernel, act=act),
        grid=(R // tm, N // tn),
        in_specs=[
            pl.BlockSpec((tm, D), lambda i, j: (i, 0)),
            pl.BlockSpec((1, D), lambda i, j: (0, 0)),
            pl.BlockSpec((D, tn), lambda i, j: (0, j)),
        ],
        out_specs=pl.BlockSpec((tm, tn), lambda i, j: (i, j)),
        out_shape=jax.ShapeDtypeStruct((R, N), out_dtype),
        scratch_shapes=[pltpu.VMEM((tm, D), BF16)],
        compiler_params=_params(("arbitrary", "arbitrary"), 48),
        name="norm_proj",
    )(x, g.reshape(1, D), w)


def _pool_kernel(u_ref, wg_ref, sc_ref, o_ref, pad_ref, *, S):
    W = u_ref.shape[1]
    pad_ref[0:POOL_HALO, :] = jnp.zeros((POOL_HALO, W), F32)
    pad_ref[POOL_HALO:POOL_HALO + S, :] = u_ref[...]
    t = lax.broadcasted_iota(jnp.int32, (S, 1), 0).astype(F32)
    for g, w in enumerate(POOL_WINDOWS):
        lo, hi = g * HEAD_DIM, (g + 1) * HEAD_DIM
        u = pad_ref[POOL_HALO:POOL_HALO + S, lo:hi]
        acc = u
        for k in range(1, w):
            acc = acc + pad_ref[POOL_HALO - k:POOL_HALO - k + S, lo:hi]
        mean = acc / jnp.minimum(t + 1.0, float(w))
        d = (mean - u).astype(BF16)
        y = jnp.dot(d, wg_ref[g], preferred_element_type=F32) * sc_ref[:, lo:hi]
        o_ref[:, lo:hi] = y.astype(o_ref.dtype)


def pool_mixer(u, w_group, scale, B, S):
    T, W = u.shape
    G = len(POOL_WINDOWS)
    assert W == G * HEAD_DIM
    return pl.pallas_call(
        functools.partial(_pool_kernel, S=S),
        grid=(B,),
        in_specs=[
            pl.BlockSpec((S, W), lambda b: (b, 0)),
            pl.BlockSpec((G, HEAD_DIM, HEAD_DIM), lambda b: (0, 0, 0)),
            pl.BlockSpec((1, W), lambda b: (0, 0)),
        ],
        out_specs=pl.BlockSpec((S, W), lambda b: (b, 0)),
        out_shape=jax.ShapeDtypeStruct((T, W), BF16),
        scratch_shapes=[pltpu.VMEM((S + POOL_HALO, W), F32)],
        compiler_params=_params(("arbitrary",), 40),
        name="pool_mixer",
    )(u, w_group.astype(BF16), scale.reshape(1, W))


SB_DEAD_LOG = -104.0


def _sb_kernel(q_ref, k_ref, v_ref, uu_ref, o_ref, *state, tq, n_heads, scale):
    acc_refs, run_refs = state[:n_heads], state[n_heads:]
    qi = pl.program_id(1)
    for ref in state:
        ref[...] = jnp.zeros_like(ref)
    uu = uu_ref[...]
    diag_strict = (lax.broadcasted_iota(jnp.int32, (tq, tq), 1)
                   < lax.broadcasted_iota(jnp.int32, (tq, tq), 0))

    def block(kb, on_diagonal):
        off = pl.multiple_of(kb * tq, tq)
        heads = [slice(h * HEAD_DIM, (h + 1) * HEAD_DIM) for h in range(n_heads)]
        zs = [lax.dot_general(q_ref[:, sl], k_ref[pl.ds(off, tq), sl], (((1,), (1,)), ((), ())),
                              preferred_element_type=F32) * scale for sl in heads]
        hits, sums = [], []
        for z in zs:
            soft = jnp.log(1.0 + jnp.exp(-jnp.abs(z)))
            log_fail = -jnp.maximum(z, 0.0) - soft
            hits.append(jnp.minimum(z, 0.0) - soft)
            if on_diagonal:
                log_fail = jnp.where(diag_strict, log_fail, 0.0)
            hi = log_fail.astype(BF16)
            lo = (log_fail - hi.astype(F32)).astype(BF16)
            sums.append(jnp.dot(jnp.concatenate([hi, lo], axis=1), uu, preferred_element_type=F32))
        worst = None
        for h, sl in enumerate(heads):
            run = run_refs[h][...]
            a = jnp.exp(hits[h] + sums[h][:, :tq] + run)
            if on_diagonal:
                a = jnp.where(diag_strict, a, 0.0)
            acc_refs[h][...] += jnp.dot(a.astype(BF16), v_ref[pl.ds(off, tq), sl],
                                      preferred_element_type=F32)
            run = run + sums[h][:, tq:]
            run_refs[h][...] = run
            worst = run if worst is None else jnp.maximum(worst, run)
        return jnp.max(worst)

    def more(c):
        it, worst = c
        return jnp.logical_and(it <= qi, worst > SB_DEAD_LOG)

    def step(c):
        it, _ = c
        return it + 1, block(qi - it, False)

    lax.while_loop(more, step, (jnp.int32(1), block(qi, True)))
    for h in range(n_heads):
        o_ref[:, h * HEAD_DIM:(h + 1) * HEAD_DIM] = acc_refs[h][...].astype(o_ref.dtype)


def sb_attention(qkv, B, S, n_heads, tq=128):
    T = qkv.shape[0]
    W = n_heads * HEAD_DIM
    assert S % tq == 0 and tq == HEAD_DIM
    nq = S // tq
    tri = (np.arange(tq)[:, None] > np.arange(tq)[None, :]).astype(np.float32)
    half = np.concatenate([tri, np.ones((tq, tq), np.float32)], axis=1)
    uu = jnp.asarray(np.concatenate([half, half], axis=0), BF16)
    return pl.pallas_call(
        functools.partial(_sb_kernel, tq=tq, n_heads=n_heads, scale=1.0 / float(np.sqrt(HEAD_DIM))),
        grid=(B, nq),
        in_specs=[
            pl.BlockSpec((tq, W), lambda b, i: (b * nq + i, 0)),
            pl.BlockSpec((S, W), lambda b, i: (b, 1)),
            pl.BlockSpec((S, W), lambda b, i: (b, 2)),
            pl.BlockSpec((2 * tq, 2 * tq), lambda b, i: (0, 0)),
        ],
        out_specs=pl.BlockSpec((tq, W), lambda b, i: (b * nq + i, 0)),
        out_shape=jax.ShapeDtypeStruct((T, W), BF16),
        scratch_shapes=[pltpu.VMEM((tq, HEAD_DIM), F32)] * (2 * n_heads),
        compiler_params=_params(("arbitrary", "arbitrary"), 40),
        name="sb_attention",
    )(qkv, qkv, qkv, uu)


def _xattn_kernel(q_ref, kv_ref, o_ref, *, n_heads, scale):
    W = n_heads * HEAD_DIM
    for h in range(n_heads):
        lo, hi = h * HEAD_DIM, (h + 1) * HEAD_DIM
        q = q_ref[:, lo:hi]
        k = kv_ref[:, lo:hi]
        v = kv_ref[:, W + lo:W + hi]
        s = lax.dot_general(q, k, (((1,), (1,)), ((), ())), preferred_element_type=F32) * scale
        m = jnp.max(s, axis=-1, keepdims=True)
        p = jnp.exp(s - m)
        p = p / jnp.sum(p, axis=-1, keepdims=True)
        o_ref[:, lo:hi] = jnp.dot(p.astype(BF16), v, preferred_element_type=F32).astype(o_ref.dtype)


def mem_attention(qkv, memkv, S, q_col_block, n_heads, tq=512):
    T = qkv.shape[0]
    W = n_heads * HEAD_DIM
    M = memkv.shape[0] // (T // S)
    assert S % tq == 0
    per = S // tq
    return pl.pallas_call(
        functools.partial(_xattn_kernel, n_heads=n_heads, scale=1.0 / float(np.sqrt(HEAD_DIM))),
        grid=(T // tq,),
        in_specs=[
            pl.BlockSpec((tq, W), lambda i: (i, q_col_block)),
            pl.BlockSpec((M, 2 * W), lambda i: (i // per, 0)),
        ],
        out_specs=pl.BlockSpec((tq, W), lambda i: (i, 0)),
        out_shape=jax.ShapeDtypeStruct((T, W), BF16),
        compiler_params=_params(("arbitrary",), 40),
        name="mem_attention",
    )(qkv, memkv)


def _merge_kernel(p_ref, sb_ref, xa_ref, g_ref, x_ref, wpp_ref, wsp_ref, wmp_ref, wout_ref,
                  gffn_ref, wr_ref, br_ref, h_ref, hn_ref, lg_ref):
    D = x_ref.shape[1]
    y_pool = jnp.dot(p_ref[...], wpp_ref[...], preferred_element_type=F32)
    y_sb = jnp.dot(sb_ref[...], wsp_ref[...], preferred_element_type=F32)
    y_xa = jnp.dot(xa_ref[...], wmp_ref[...], preferred_element_type=F32)
    merged = g_ref[:, 0:D] * y_pool + g_ref[:, D:2 * D] * y_sb + g_ref[:, 2 * D:3 * D] * y_xa
    h = x_ref[...] + jnp.dot(merged.astype(BF16), wout_ref[...], preferred_element_type=F32)
    h_ref[...] = h
    hn = _rms(h, gffn_ref[...])
    hn_ref[...] = hn
    lg_ref[...] = jnp.dot(hn.astype(BF16), wr_ref[...], preferred_element_type=F32) + br_ref[...]


def merge_block(p, a_sb, a_xa, gates, x, wpp, wsp, wmp, wout, g_ffn, w_router, b_router, tm=256):
    T, D = x.shape
    E = w_router.shape[1]
    wr = jnp.zeros((D, LANES), BF16).at[:, :E].set(w_router.astype(BF16))
    br = jnp.zeros((1, LANES), F32).at[0, :E].set(b_router)
    row = lambda w: pl.BlockSpec((tm, w), lambda i: (i, 0))
    full = lambda a: pl.BlockSpec(a.shape, lambda i: (0,) * a.ndim, pipeline_mode=pl.Buffered(1))
    weights = [wpp.astype(BF16), wsp.astype(BF16), wmp.astype(BF16), wout.astype(BF16),
               g_ffn.reshape(1, D), wr, br]
    return pl.pallas_call(
        _merge_kernel,
        grid=(T // tm,),
        in_specs=[row(p.shape[1]), row(a_sb.shape[1]), row(a_xa.shape[1]), row(3 * D), row(D)]
                 + [full(a) for a in weights],
        out_specs=[row(D), row(D), row(LANES)],
        out_shape=[jax.ShapeDtypeStruct((T, D), F32), jax.ShapeDtypeStruct((T, D), F32),
                   jax.ShapeDtypeStruct((T, LANES), F32)],
        compiler_params=_params(("arbitrary",), 56),
        name="merge_block",
    )(p, a_sb, a_xa, gates, x, *weights)


def _route_kernel(lg_ref, tri_ref, idx_ref, prob_ref, rank_ref, cnt_ref, carry_ref, *, n_experts):
    @pl.when(pl.program_id(0) == 0)
    def _():
        carry_ref[...] = jnp.zeros_like(carry_ref)

    tr = lg_ref.shape[0]
    lane = lax.broadcasted_iota(jnp.int32, (tr, LANES), 1)
    logit = jnp.where(lane < n_experts, lg_ref[...], -jnp.inf)
    onehot = jnp.zeros((tr, LANES), F32)
    vals, idxs = [], []
    for _ in range(TOP_K):
        m = jnp.max(logit, axis=-1, keepdims=True)
        ik = jnp.min(jnp.where(logit == m, lane, LANES), axis=-1, keepdims=True)
        sel = lane == ik
        onehot = jnp.where(sel, 1.0, onehot)
        logit = jnp.where(sel, -jnp.inf, logit)
        vals.append(m)
        idxs.append(ik)
    es = [jnp.exp(v - vals[0]) for v in vals]
    den = es[0] + es[1] + es[2] + es[3]
    before = jnp.dot(tri_ref[...], onehot.astype(BF16), preferred_element_type=F32) + carry_ref[...]
    carry_ref[...] = carry_ref[...] + jnp.sum(onehot, axis=0, keepdims=True)
    cnt_ref[...] = carry_ref[...]
    idx_out = jnp.zeros((tr, LANES), jnp.int32)
    prob_out = jnp.zeros((tr, LANES), F32)
    rank_out = jnp.zeros((tr, LANES), F32)
    for k in range(TOP_K):
        rk = jnp.sum(jnp.where(lane == idxs[k], before, 0.0), axis=-1, keepdims=True)
        idx_out = jnp.where(lane == k, idxs[k], idx_out)
        prob_out = jnp.where(lane == k, es[k] / den, prob_out)
        rank_out = jnp.where(lane == k, rk, rank_out)
    idx_ref[...] = idx_out
    prob_ref[...] = prob_out
    rank_ref[...] = rank_out.astype(jnp.int32)


def route(logits, n_experts, tr=256):
    T = logits.shape[0]
    tri = jnp.asarray((np.arange(tr)[None, :] < np.arange(tr)[:, None]).astype(np.float32), BF16)
    row = pl.BlockSpec((tr, LANES), lambda i: (i, 0))
    return pl.pallas_call(
        functools.partial(_route_kernel, n_experts=n_experts),
        grid=(T // tr,),
        in_specs=[row, pl.BlockSpec((tr, tr), lambda i: (0, 0))],
        out_specs=[row, row, row, pl.BlockSpec((1, LANES), lambda i: (0, 0))],
        out_shape=[jax.ShapeDtypeStruct((T, LANES), jnp.int32), jax.ShapeDtypeStruct((T, LANES), F32),
                   jax.ShapeDtypeStruct((T, LANES), jnp.int32), jax.ShapeDtypeStruct((1, LANES), F32)],
        scratch_shapes=[pltpu.VMEM((1, LANES), F32)],
        compiler_params=_params(("arbitrary",), 32),
        name="route",
    )(logits, tri)


INVERT_STEPS = 16


def _invert_kernel(dest_ref, o_ref):
    phase, s = pl.program_id(0), pl.program_id(1)
    rows_per_step = o_ref.shape[0] // INVERT_STEPS
    toks_per_step = dest_ref.shape[0] // TOP_K // INVERT_STEPS

    @pl.when(phase == 0)
    def _():
        def clear(r, c):
            o_ref[s * rows_per_step + r] = 0
            return c

        lax.fori_loop(0, rows_per_step, clear, 0, unroll=8)

    @pl.when(phase == 1)
    def _():
        def put(j, c):
            t = s * toks_per_step + j
            for k in range(TOP_K):
                o_ref[dest_ref[t * TOP_K + k]] = t
            return c

        lax.fori_loop(0, toks_per_step, put, 0, unroll=4)


def invert_routing(dest_flat, n_rows):
    assert n_rows % (8 * INVERT_STEPS) == 0 and dest_flat.shape[0] % (4 * TOP_K * INVERT_STEPS) == 0
    return pl.pallas_call(
        _invert_kernel,
        grid=(2, INVERT_STEPS),
        in_specs=[pl.BlockSpec(memory_space=pltpu.SMEM)],
        out_specs=pl.BlockSpec(memory_space=pltpu.SMEM),
        out_shape=jax.ShapeDtypeStruct((n_rows,), jnp.int32),
        compiler_params=pltpu.CompilerParams(dimension_semantics=("arbitrary", "arbitrary")),
        name="invert_routing",
    )(dest_flat)


def _expert_changed(i, be_ref):
    return jnp.logical_or(i == 0, be_ref[i] != be_ref[jnp.maximum(i - 1, 0)])


def _moe_up_kernel(be_ref, nu_ref, x_ref, wg_ref, wu_ref, bg_ref, bu_ref, o_ref, wg_bf, wu_bf):
    i = pl.program_id(1)

    @pl.when(i < nu_ref[0])
    def _():
        @pl.when(_expert_changed(i, be_ref))
        def _():
            wg_bf[...] = wg_ref[0].astype(BF16)
            wu_bf[...] = wu_ref[0].astype(BF16)

        o_ref[...] = _swiglu(x_ref[...], wg_bf, wu_bf, bg_ref, bu_ref).astype(o_ref.dtype)

    @pl.when(i >= nu_ref[0])
    def _():
        o_ref[...] = jnp.zeros_like(o_ref)


def _swiglu(x, wg_bf, wu_bf, bg_ref, bu_ref):
    gate = jnp.dot(x, wg_bf[...], preferred_element_type=F32) + bg_ref[0]
    up = jnp.dot(x, wu_bf[...], preferred_element_type=F32) + bu_ref[0]
    gate = jnp.minimum(gate, SWIGLU_LIMIT)
    up = jnp.clip(up, -SWIGLU_LIMIT, SWIGLU_LIMIT)
    return gate * jax.nn.sigmoid(SWIGLU_ALPHA * gate) * (up + 1.0)


def _moe_up_gather_kernel(tok_ref, be_ref, nu_ref, src_ref, wg_ref, wu_ref, bg_ref, bu_ref,
                          o_ref, xs_ref, wg_bf, wu_bf, buf0, buf1, sem0, sem1, *, tm):
    i = pl.program_id(0)
    nu = nu_ref[0]

    def issue(blk, buf, sem):
        for r in range(tm):
            t = tok_ref[blk * tm + r]
            pltpu.make_async_copy(src_ref.at[pl.ds(t, 1)], buf.at[pl.ds(r, 1)], sem).start(
                priority=r % 2)

    def wait(buf, sem):
        pltpu.make_async_copy(src_ref.at[pl.ds(0, tm)], buf, sem).wait()

    @pl.when(i == 0)
    def _():
        issue(0, buf0, sem0)

    def work(buf, sem, next_buf, next_sem):
        @pl.when(_expert_changed(i, be_ref))
        def _():
            wg_bf[...] = wg_ref[0].astype(BF16)
            wu_bf[...] = wu_ref[0].astype(BF16)

        wait(buf, sem)
        x = buf[...].astype(BF16)
        xs_ref[...] = x
        o_ref[...] = _swiglu(x, wg_bf, wu_bf, bg_ref, bu_ref).astype(o_ref.dtype)
        issue(i + 1, next_buf, next_sem)

    @pl.when(jnp.logical_and(i < nu, i % 2 == 0))
    def _():
        work(buf0, sem0, buf1, sem1)

    @pl.when(jnp.logical_and(i < nu, i % 2 == 1))
    def _():
        work(buf1, sem1, buf0, sem0)

    @pl.when(i >= nu)
    def _():
        o_ref[...] = jnp.zeros_like(o_ref)
        xs_ref[...] = jnp.zeros_like(xs_ref)

    @pl.when(jnp.logical_and(i == nu, nu % 2 == 0))
    def _():
        wait(buf0, sem0)

    @pl.when(jnp.logical_and(i == nu, nu % 2 == 1))
    def _():
        wait(buf1, sem1)


def moe_up_gather(rows_tok, blk_expert, n_used, src, w_gate, b_gate, w_up, b_up, tm, tn=1024):
    n_rows = rows_tok.shape[0]
    D = src.shape[1]
    E, _, F = w_gate.shape
    w_spec = pl.BlockSpec((1, D, tn), lambda i, tok, be, nu: (be[_blk(i, nu)], 0, 0))
    b_spec = pl.BlockSpec((1, 1, tn), lambda i, tok, be, nu: (be[_blk(i, nu)], 0, 0))
    return pl.pallas_call(
        functools.partial(_moe_up_gather_kernel, tm=tm),
        grid_spec=pltpu.PrefetchScalarGridSpec(
            num_scalar_prefetch=3,
            grid=(n_rows // tm,),
            in_specs=[pl.BlockSpec(memory_space=pl.ANY), w_spec, w_spec, b_spec, b_spec],
            out_specs=[pl.BlockSpec((tm, tn), lambda i, tok, be, nu: (i, 0)),
                       pl.BlockSpec((tm, D), lambda i, tok, be, nu: (i, 0))],
            scratch_shapes=[pltpu.VMEM((D, tn), BF16), pltpu.VMEM((D, tn), BF16),
                            pltpu.VMEM((tm, D), F32), pltpu.VMEM((tm, D), F32),
                            pltpu.SemaphoreType.DMA(()), pltpu.SemaphoreType.DMA(())],
        ),
        out_shape=[jax.ShapeDtypeStruct((n_rows, tn), BF16), jax.ShapeDtypeStruct((n_rows, D), BF16)],
        compiler_params=_params(("arbitrary",), 56, row_dma=True),
        name="moe_up_gather",
    )(rows_tok, blk_expert, n_used, src, w_gate, w_up, b_gate.reshape(E, 1, F), b_up.reshape(E, 1, F))


def _moe_down_kernel(be_ref, nu_ref, h0_ref, h1_ref, wd_ref, bd_ref, o_ref, wd_bf):
    i = pl.program_id(1)
    F0 = h0_ref.shape[1]

    @pl.when(i < nu_ref[0])
    def _():
        @pl.when(_expert_changed(i, be_ref))
        def _():
            wd_bf[...] = wd_ref[0].astype(BF16)

        o_ref[...] = (jnp.dot(h0_ref[...], wd_bf[:F0, :], preferred_element_type=F32)
                      + jnp.dot(h1_ref[...], wd_bf[F0:, :], preferred_element_type=F32) + bd_ref[0])

    @pl.when(i >= nu_ref[0])
    def _():
        o_ref[...] = jnp.zeros_like(o_ref)


def _blk(i, nu):
    return jnp.minimum(i, nu[0] - 1)


def moe_up(blk_expert, n_used, xs, w_gate, b_gate, w_up, b_up, tm, tn=1024):
    n_rows, D = xs.shape
    E, _, F = w_gate.shape
    w_spec = pl.BlockSpec((1, D, tn), lambda j, i, be, nu: (be[_blk(i, nu)], 0, j + 1))
    b_spec = pl.BlockSpec((1, 1, tn), lambda j, i, be, nu: (be[_blk(i, nu)], 0, j + 1))
    return pl.pallas_call(
        _moe_up_kernel,
        grid_spec=pltpu.PrefetchScalarGridSpec(
            num_scalar_prefetch=2,
            grid=(F // tn - 1, n_rows // tm),
            in_specs=[pl.BlockSpec((tm, D), lambda j, i, be, nu: (_blk(i, nu), 0)),
                      w_spec, w_spec, b_spec, b_spec],
            out_specs=pl.BlockSpec((tm, tn), lambda j, i, be, nu: (i, j)),
            scratch_shapes=[pltpu.VMEM((D, tn), BF16), pltpu.VMEM((D, tn), BF16)],
        ),
        out_shape=jax.ShapeDtypeStruct((n_rows, F - tn), BF16),
        compiler_params=_params(("arbitrary", "arbitrary"), 56),
        name="moe_up",
    )(blk_expert, n_used, xs, w_gate, w_up, b_gate.reshape(E, 1, F), b_up.reshape(E, 1, F))


def moe_down(blk_expert, n_used, hs0, hs1, w_down, b_down, tm, tn=2048):
    n_rows, F0 = hs0.shape
    F1 = hs1.shape[1]
    F = F0 + F1
    E, _, D = w_down.shape
    tn = min(tn, D)
    return pl.pallas_call(
        _moe_down_kernel,
        grid_spec=pltpu.PrefetchScalarGridSpec(
            num_scalar_prefetch=2,
            grid=(D // tn, n_rows // tm),
            in_specs=[pl.BlockSpec((tm, F0), lambda j, i, be, nu: (_blk(i, nu), 0)),
                      pl.BlockSpec((tm, F1), lambda j, i, be, nu: (_blk(i, nu), 0)),
                      pl.BlockSpec((1, F, tn), lambda j, i, be, nu: (be[_blk(i, nu)], 0, j)),
                      pl.BlockSpec((1, 1, tn), lambda j, i, be, nu: (be[_blk(i, nu)], 0, j))],
            out_specs=pl.BlockSpec((tm, tn), lambda j, i, be, nu: (i, j)),
            scratch_shapes=[pltpu.VMEM((F, tn), BF16)],
        ),
        out_shape=jax.ShapeDtypeStruct((n_rows, D), F32),
        compiler_params=_params(("arbitrary", "arbitrary"), 56),
        name="moe_down",
    )(blk_expert, n_used, hs0, hs1, w_down, b_down.reshape(E, 1, D))


def _combine_kernel(dest_ref, y_ref, p_ref, h_ref, g_ref, o_ref, buf0, buf1, sem0, sem1, *,
                    tc, final_norm):
    i = pl.program_id(0)

    def issue(blk, buf, sem):
        def rows(r, c):
            for k in range(TOP_K):
                d = dest_ref[(blk * tc + r) * TOP_K + k]
                pltpu.make_async_copy(y_ref.at[pl.ds(d, 1)], buf.at[k, pl.ds(r, 1)], sem).start(
                    priority=k % 2)
            return c

        lax.fori_loop(0, tc, rows, 0, unroll=4)

    @pl.when(i == 0)
    def _():
        issue(0, buf0, sem0)

    def work(buf, sem, next_buf, next_sem):
        @pl.when(i + 1 < pl.num_programs(0))
        def _():
            issue(i + 1, next_buf, next_sem)

        for k in range(TOP_K):
            pltpu.make_async_copy(y_ref.at[pl.ds(0, tc)], buf.at[k], sem).wait()
        p = p_ref[...]
        h = h_ref[...]
        for k in range(TOP_K):
            h = h + p[:, k:k + 1] * buf[k]
        o_ref[...] = _rms(h, g_ref[...]) if final_norm else h

    @pl.when(i % 2 == 0)
    def _():
        work(buf0, sem0, buf1, sem1)

    @pl.when(i % 2 == 1)
    def _():
        work(buf1, sem1, buf0, sem0)


def combine(dest_flat, y, probs, h, g_final, final_norm, tc=128):
    T, D = h.shape
    return pl.pallas_call(
        functools.partial(_combine_kernel, tc=tc, final_norm=final_norm),
        grid_spec=pltpu.PrefetchScalarGridSpec(
            num_scalar_prefetch=1,
            grid=(T // tc,),
            in_specs=[pl.BlockSpec(memory_space=pl.ANY),
                      pl.BlockSpec((tc, LANES), lambda i, d: (i, 0)),
                      pl.BlockSpec((tc, D), lambda i, d: (i, 0)),
                      pl.BlockSpec((1, D), lambda i, d: (0, 0))],
            out_specs=pl.BlockSpec((tc, D), lambda i, d: (i, 0)),
            scratch_shapes=[pltpu.VMEM((TOP_K, tc, D), F32), pltpu.VMEM((TOP_K, tc, D), F32),
                            pltpu.SemaphoreType.DMA(()), pltpu.SemaphoreType.DMA(())],
        ),
        out_shape=jax.ShapeDtypeStruct((T, D), F32),
        compiler_params=_params(("arbitrary",), 32, row_dma=True),
        name="combine",
    )(dest_flat, y, probs, h, g_final.reshape(1, D))


MOE_ROW_BLOCK = 256


def kernel(x, mem, norm_mix_g, norm_mem_g, w_in, w_pool_group, pool_scale, w_pool_proj, w_sb_proj,
           w_mem_kv, w_mem_proj, w_out, norm_ffn_g, w_router, b_router, w_gate, b_gate, w_up, b_up,
           w_down, b_down, norm_final_g):
    B, S, D = x.shape
    M = mem.shape[1]
    T = B * S
    depth = w_in.shape[0]
    pool_w = w_pool_proj.shape[1]
    sb_w = w_sb_proj.shape[1]
    xa_w = w_mem_proj.shape[1]
    E = w_router.shape[2]
    sb_heads = sb_w // HEAD_DIM
    xa_heads = xa_w // HEAD_DIM
    qkv_w = 3 * sb_w + xa_w
    assert (3 * sb_w) % xa_w == 0

    h = x.reshape(T, D)
    mem2 = mem.reshape(B * M, D)
    for l in range(depth):
        w_in_bf = w_in[l].astype(BF16)
        u_pool = norm_proj(h, norm_mix_g[l], w_in_bf[:, :pool_w], F32)
        qkv = norm_proj(h, norm_mix_g[l], w_in_bf[:, pool_w:pool_w + qkv_w], BF16, tn=qkv_w // 2)
        gates = norm_proj(h, norm_mix_g[l], w_in_bf[:, pool_w + qkv_w:], F32, act="sigmoid")
        memkv = norm_proj(mem2, norm_mem_g[l], w_mem_kv[l].astype(BF16), BF16)
        p = pool_mixer(u_pool, w_pool_group[l], pool_scale[l], B, S)
        a_sb = sb_attention(qkv, B, S, sb_heads)
        a_xa = mem_attention(qkv, memkv, S, (3 * sb_w) // xa_w, xa_heads)
        h, hn, logits = merge_block(p, a_sb, a_xa, gates, h, w_pool_proj[l], w_sb_proj[l],
                                    w_mem_proj[l], w_out[l], norm_ffn_g[l], w_router[l], b_router[l])
        idx, probs, rank, counts = route(logits, E)
        tm = MOE_ROW_BLOCK
        n_blk = (T * TOP_K) // tm + E + 1
        counts = counts[0, :E].astype(jnp.int32)
        padded = (counts + tm - 1) // tm * tm
        pad_end = jnp.cumsum(padded)
        pad_start = pad_end - padded
        dest = (pad_start[idx[:, :TOP_K]] + rank[:, :TOP_K]).reshape(-1)
        rows_tok = invert_routing(dest, n_blk * tm)
        blk_start = jnp.arange(n_blk, dtype=jnp.int32) * tm
        blk_expert = jnp.minimum(
            jnp.sum((pad_end[None, :] <= blk_start[:, None]).astype(jnp.int32), axis=1), E - 1)
        n_used = (pad_end[-1:] // tm).astype(jnp.int32)
        hs0, xs = moe_up_gather(rows_tok, blk_expert, n_used, hn, w_gate[l], b_gate[l], w_up[l],
                                b_up[l], tm)
        hs1 = moe_up(blk_expert, n_used, xs, w_gate[l], b_gate[l], w_up[l], b_up[l], tm)
        ys = moe_down(blk_expert, n_used, hs0, hs1, w_down[l], b_down[l], tm)
        h = combine(dest, ys, probs, h, norm_final_g,
                    final_norm=(l == depth - 1))
    return h.reshape(B, S, D)
```

```python
import functools

import numpy as np
import jax
import jax.numpy as jnp
from jax import lax
from jax.experimental import pallas as pl
from jax.experimental.pallas import tpu as pltpu

EPS = 1e-5
HEAD_DIM = 128
POOL_WINDOWS = (2, 4, 8, 16)
POOL_HALO = 16
TOP_K = 4
SWIGLU_LIMIT = 7.0
SWIGLU_ALPHA = 1.702
LANES = 128

BF16 = jnp.bfloat16
F32 = jnp.float32

MIB = 1024 * 1024


def _params(sem, vmem_mib, row_dma=False):
    return pltpu.CompilerParams(dimension_semantics=sem, vmem_limit_bytes=vmem_mib * MIB,
                                disable_bounds_checks=row_dma)


def _rms(x, g):
    ms = jnp.mean(x * x, axis=-1, keepdims=True)
    return x * lax.rsqrt(ms + EPS) * g


def _norm_proj_kernel(x_ref, g_ref, w_ref, o_ref, xn_ref, *, act):
    @pl.when(pl.program_id(1) == 0)
    def _():
        xn_ref[...] = _rms(x_ref[...], g_ref[...]).astype(BF16)

    y = jnp.dot(xn_ref[...], w_ref[...], preferred_element_type=F32)
    if act == "sigmoid":
        y = jax.nn.sigmoid(y)
    o_ref[...] = y.astype(o_ref.dtype)


def norm_proj(x, g, w, out_dtype, act=None, tm=1024, tn=1024):
    R, D = x.shape
    N = w.shape[1]
    tm = min(tm, R)
    tn = min(tn, N)
    assert R % tm == 0 and N % tn == 0
    return pl.pallas_call(
        functools.partial(_norm_proj_kernel, act=act),
        grid=(R // tm, N // tn),
        in_specs=[
            pl.BlockSpec((tm, D), lambda i, j: (i, 0)),
            pl.BlockSpec((1, D), lambda i, j: (0, 0)),
            pl.BlockSpec((D, tn), lambda i, j: (0, j)),
        ],
        out_specs=pl.BlockSpec((tm, tn), lambda i, j: (i, j)),
        out_shape=jax.ShapeDtypeStruct((R, N), out_dtype),
        scratch_shapes=[pltpu.VMEM((tm, D), BF16)],
        compiler_params=_params(("arbitrary", "arbitrary"), 48),
        name="norm_proj",
    )(x, g.reshape(1, D), w)


def _pool_kernel(u_ref, wg_ref, sc_ref, o_ref, pad_ref, *, S):
    W = u_ref.shape[1]
    pad_ref[0:POOL_HALO, :] = jnp.zeros((POOL_HALO, W), F32)
    pad_ref[POOL_HALO:POOL_HALO + S, :] = u_ref[...]
    t = lax.broadcasted_iota(jnp.int32, (S, 1), 0).astype(F32)
    for g, w in enumerate(POOL_WINDOWS):
        lo, hi = g * HEAD_DIM, (g + 1) * HEAD_DIM
        u = pad_ref[POOL_HALO:POOL_HALO + S, lo:hi]
        acc = u
        for k in range(1, w):
            acc = acc + pad_ref[POOL_HALO - k:POOL_HALO - k + S, lo:hi]
        mean = acc / jnp.minimum(t + 1.0, float(w))
        d = (mean - u).astype(BF16)
        y = jnp.dot(d, wg_ref[g], preferred_element_type=F32) * sc_ref[:, lo:hi]
        o_ref[:, lo:hi] = y.astype(o_ref.dtype)


def pool_mixer(u, w_group, scale, B, S):
    T, W = u.shape
    G = len(POOL_WINDOWS)
    assert W == G * HEAD_DIM
    return pl.pallas_call(
        functools.partial(_pool_kernel, S=S),
        grid=(B,),
        in_specs=[
            pl.BlockSpec((S, W), lambda b: (b, 0)),
            pl.BlockSpec((G, HEAD_DIM, HEAD_DIM), lambda b: (0, 0, 0)),
            pl.BlockSpec((1, W), lambda b: (0, 0)),
        ],
        out_specs=pl.BlockSpec((S, W), lambda b: (b, 0)),
        out_shape=jax.ShapeDtypeStruct((T, W), BF16),
        scratch_shapes=[pltpu.VMEM((S + POOL_HALO, W), F32)],
        compiler_params=_params(("arbitrary",), 40),
        name="pool_mixer",
    )(u, w_group.astype(BF16), scale.reshape(1, W))


SB_DEAD_LOG = -104.0


def _sb_kernel(q_ref, k_ref, v_ref, uu_ref, o_ref, *state, tq, n_heads, scale):
    acc_refs, run_refs = state[:n_heads], state[n_heads:]
    qi = pl.program_id(1)
    for ref in state:
        ref[...] = jnp.zeros_like(ref)
    uu = uu_ref[...]
    diag_strict = (lax.broadcasted_iota(jnp.int32, (tq, tq), 1)
                   < lax.broadcasted_iota(jnp.int32, (tq, tq), 0))

    def block(kb, on_diagonal):
        off = pl.multiple_of(kb * tq, tq)
        heads = [slice(h * HEAD_DIM, (h + 1) * HEAD_DIM) for h in range(n_heads)]
        zs = [lax.dot_general(q_ref[:, sl], k_ref[pl.ds(off, tq), sl], (((1,), (1,)), ((), ())),
                              preferred_element_type=F32) * scale for sl in heads]
        hits, sums = [], []
        for z in zs:
            soft = jnp.log(1.0 + jnp.exp(-jnp.abs(z)))
            log_fail = -jnp.maximum(z, 0.0) - soft
            hits.append(jnp.minimum(z, 0.0) - soft)
            if on_diagonal:
                log_fail = jnp.where(diag_strict, log_fail, 0.0)
            hi = log_fail.astype(BF16)
            lo = (log_fail - hi.astype(F32)).astype(BF16)
            sums.append(jnp.dot(jnp.concatenate([hi, lo], axis=1), uu, preferred_element_type=F32))
        worst = None
        for h, sl in enumerate(heads):
            run = run_refs[h][...]
            a = jnp.exp(hits[h] + sums[h][:, :tq] + run)
            if on_diagonal:
                a = jnp.where(diag_strict, a, 0.0)
            acc_refs[h][...] += jnp.dot(a.astype(BF16), v_ref[pl.ds(off, tq), sl],
                                      preferred_element_type=F32)
            run = run + sums[h][:, tq:]
            run_refs[h][...] = run
            worst = run if worst is None else jnp.maximum(worst, run)
        return jnp.max(worst)

    def more(c):
        it, worst = c
        return jnp.logical_and(it <= qi, worst > SB_DEAD_LOG)

    def step(c):
        it, _ = c
        return it + 1, block(qi - it, False)

    lax.while_loop(more, step, (jnp.int32(1), block(qi, True)))
    for h in range(n_heads):
        o_ref[:, h * HEAD_DIM:(h + 1) * HEAD_DIM] = acc_refs[h][...].astype(o_ref.dtype)


def sb_attention(qkv, B, S, n_heads, tq=128):
    T = qkv.shape[0]
    W = n_heads * HEAD_DIM
    assert S % tq == 0 and tq == HEAD_DIM
    nq = S // tq
    tri = (np.arange(tq)[:, None] > np.arange(tq)[None, :]).astype(np.float32)
    half = np.concatenate([tri, np.ones((tq, tq), np.float32)], axis=1)
    uu = jnp.asarray(np.concatenate([half, half], axis=0), BF16)
    return pl.pallas_call(
        functools.partial(_sb_kernel, tq=tq, n_heads=n_heads, scale=1.0 / float(np.sqrt(HEAD_DIM))),
        grid=(B, nq),
        in_specs=[
            pl.BlockSpec((tq, W), lambda b, i: (b * nq + i, 0)),
            pl.BlockSpec((S, W), lambda b, i: (b, 1)),
            pl.BlockSpec((S, W), lambda b, i: (b, 2)),
            pl.BlockSpec((2 * tq, 2 * tq), lambda b, i: (0, 0)),
        ],
        out_specs=pl.BlockSpec((tq, W), lambda b, i: (b * nq + i, 0)),
        out_shape=jax.ShapeDtypeStruct((T, W), BF16),
        scratch_shapes=[pltpu.VMEM((tq, HEAD_DIM), F32)] * (2 * n_heads),
        compiler_params=_params(("arbitrary", "arbitrary"), 40),
        name="sb_attention",
    )(qkv, qkv, qkv, uu)


def _xattn_kernel(q_ref, kv_ref, o_ref, *, n_heads, scale):
    W = n_heads * HEAD_DIM
    for h in range(n_heads):
        lo, hi = h * HEAD_DIM, (h + 1) * HEAD_DIM
        q = q_ref[:, lo:hi]
        k = kv_ref[:, lo:hi]
        v = kv_ref[:, W + lo:W + hi]
        s = lax.dot_general(q, k, (((1,), (1,)), ((), ())), preferred_element_type=F32) * scale
        m = jnp.max(s, axis=-1, keepdims=True)
        p = jnp.exp(s - m)
        p = p / jnp.sum(p, axis=-1, keepdims=True)
        o_ref[:, lo:hi] = jnp.dot(p.astype(BF16), v, preferred_element_type=F32).astype(o_ref.dtype)


def mem_attention(qkv, memkv, S, q_col_block, n_heads, tq=512):
    T = qkv.shape[0]
    W = n_heads * HEAD_DIM
    M = memkv.shape[0] // (T // S)
    assert S % tq == 0
    per = S // tq
    return pl.pallas_call(
        functools.partial(_xattn_kernel, n_heads=n_heads, scale=1.0 / float(np.sqrt(HEAD_DIM))),
        grid=(T // tq,),
        in_specs=[
            pl.BlockSpec((tq, W), lambda i: (i, q_col_block)),
            pl.BlockSpec((M, 2 * W), lambda i: (i // per, 0)),
        ],
        out_specs=pl.BlockSpec((tq, W), lambda i: (i, 0)),
        out_shape=jax.ShapeDtypeStruct((T, W), BF16),
        compiler_params=_params(("arbitrary",), 40),
        name="mem_attention",
    )(qkv, memkv)


def _merge_kernel(p_ref, sb_ref, xa_ref, g_ref, x_ref, wpp_ref, wsp_ref, wmp_ref, wout_ref,
                  gffn_ref, wr_ref, br_ref, h_ref, hn_ref, lg_ref):
    D = x_ref.shape[1]
    y_pool = jnp.dot(p_ref[...], wpp_ref[...], preferred_element_type=F32)
    y_sb = jnp.dot(sb_ref[...], wsp_ref[...], preferred_element_type=F32)
    y_xa = jnp.dot(xa_ref[...], wmp_ref[...], preferred_element_type=F32)
    merged = g_ref[:, 0:D] * y_pool + g_ref[:, D:2 * D] * y_sb + g_ref[:, 2 * D:3 * D] * y_xa
    h = x_ref[...] + jnp.dot(merged.astype(BF16), wout_ref[...], preferred_element_type=F32)
    h_ref[...] = h
    hn = _rms(h, gffn_ref[...])
    hn_ref[...] = hn
    lg_ref[...] = jnp.dot(hn.astype(BF16), wr_ref[...], preferred_element_type=F32) + br_ref[...]


def merge_block(p, a_sb, a_xa, gates, x, wpp, wsp, wmp, wout, g_ffn, w_router, b_router, tm=256):
    T, D = x.shape
    E = w_router.shape[1]
    wr = jnp.zeros((D, LANES), BF16).at[:, :E].set(w_router.astype(BF16))
    br = jnp.zeros((1, LANES), F32).at[0, :E].set(b_router)
    row = lambda w: pl.BlockSpec((tm, w), lambda i: (i, 0))
    full = lambda a: pl.BlockSpec(a.shape, lambda i: (0,) * a.ndim, pipeline_mode=pl.Buffered(1))
    weights = [wpp.astype(BF16), wsp.astype(BF16), wmp.astype(BF16), wout.astype(BF16),
               g_ffn.reshape(1, D), wr, br]
    return pl.pallas_call(
        _merge_kernel,
        grid=(T // tm,),
        in_specs=[row(p.shape[1]), row(a_sb.shape[1]), row(a_xa.shape[1]), row(3 * D), row(D)]
                 + [full(a) for a in weights],
        out_specs=[row(D), row(D), row(LANES)],
        out_shape=[jax.ShapeDtypeStruct((T, D), F32), jax.ShapeDtypeStruct((T, D), F32),
                   jax.ShapeDtypeStruct((T, LANES), F32)],
        compiler_params=_params(("arbitrary",), 56),
        name="merge_block",
    )(p, a_sb, a_xa, gates, x, *weights)


def _route_kernel(lg_ref, tri_ref, idx_ref, prob_ref, rank_ref, cnt_ref, carry_ref, *, n_experts):
    @pl.when(pl.program_id(0) == 0)
    def _():
        carry_ref[...] = jnp.zeros_like(carry_ref)

    tr = lg_ref.shape[0]
    lane = lax.broadcasted_iota(jnp.int32, (tr, LANES), 1)
    logit = jnp.where(lane < n_experts, lg_ref[...], -jnp.inf)
    onehot = jnp.zeros((tr, LANES), F32)
    vals, idxs = [], []
    for _ in range(TOP_K):
        m = jnp.max(logit, axis=-1, keepdims=True)
        ik = jnp.min(jnp.where(logit == m, lane, LANES), axis=-1, keepdims=True)
        sel = lane == ik
        onehot = jnp.where(sel, 1.0, onehot)
        logit = jnp.where(sel, -jnp.inf, logit)
        vals.append(m)
        idxs.append(ik)
    es = [jnp.exp(v - vals[0]) for v in vals]
    den = es[0] + es[1] + es[2] + es[3]
    before = jnp.dot(tri_ref[...], onehot.astype(BF16), preferred_element_type=F32) + carry_ref[...]
    carry_ref[...] = carry_ref[...] + jnp.sum(onehot, axis=0, keepdims=True)
    cnt_ref[...] = carry_ref[...]
    idx_out = jnp.zeros((tr, LANES), jnp.int32)
    prob_out = jnp.zeros((tr, LANES), F32)
    rank_out = jnp.zeros((tr, LANES), F32)
    for k in range(TOP_K):
        rk = jnp.sum(jnp.where(lane == idxs[k], before, 0.0), axis=-1, keepdims=True)
        idx_out = jnp.where(lane == k, idxs[k], idx_out)
        prob_out = jnp.where(lane == k, es[k] / den, prob_out)
        rank_out = jnp.where(lane == k, rk, rank_out)
    idx_ref[...] = idx_out
    prob_ref[...] = prob_out
    rank_ref[...] = rank_out.astype(jnp.int32)


def route(logits, n_experts, tr=256):
    T = logits.shape[0]
    tri = jnp.asarray((np.arange(tr)[None, :] < np.arange(tr)[:, None]).astype(np.float32), BF16)
    row = pl.BlockSpec((tr, LANES), lambda i: (i, 0))
    return pl.pallas_call(
        functools.partial(_route_kernel, n_experts=n_experts),
        grid=(T // tr,),
        in_specs=[row, pl.BlockSpec((tr, tr), lambda i: (0, 0))],
        out_specs=[row, row, row, pl.BlockSpec((1, LANES), lambda i: (0, 0))],
        out_shape=[jax.ShapeDtypeStruct((T, LANES), jnp.int32), jax.ShapeDtypeStruct((T, LANES), F32),
                   jax.ShapeDtypeStruct((T, LANES), jnp.int32), jax.ShapeDtypeStruct((1, LANES), F32)],
        scratch_shapes=[pltpu.VMEM((1, LANES), F32)],
        compiler_params=_params(("arbitrary",), 32),
        name="route",
    )(logits, tri)


INVERT_STEPS = 16


def _invert_kernel(dest_ref, o_ref):
    phase, s = pl.program_id(0), pl.program_id(1)
    rows_per_step = o_ref.shape[0] // INVERT_STEPS
    toks_per_step = dest_ref.shape[0] // TOP_K // INVERT_STEPS

    @pl.when(phase == 0)
    def _():
        def clear(r, c):
            o_ref[s * rows_per_step + r] = 0
            return c

        lax.fori_loop(0, rows_per_step, clear, 0, unroll=8)

    @pl.when(phase == 1)
    def _():
        def put(j, c):
            t = s * toks_per_step + j
            for k in range(TOP_K):
                o_ref[dest_ref[t * TOP_K + k]] = t
            return c

        lax.fori_loop(0, toks_per_step, put, 0, unroll=4)


def invert_routing(dest_flat, n_rows):
    assert n_rows % (8 * INVERT_STEPS) == 0 and dest_flat.shape[0] % (4 * TOP_K * INVERT_STEPS) == 0
    return pl.pallas_call(
        _invert_kernel,
        grid=(2, INVERT_STEPS),
        in_specs=[pl.BlockSpec(memory_space=pltpu.SMEM)],
        out_specs=pl.BlockSpec(memory_space=pltpu.SMEM),
        out_shape=jax.ShapeDtypeStruct((n_rows,), jnp.int32),
        compiler_params=pltpu.CompilerParams(dimension_semantics=("arbitrary", "arbitrary")),
        name="invert_routing",
    )(dest_flat)


def _expert_changed(i, be_ref):
    return jnp.logical_or(i == 0, be_ref[i] != be_ref[jnp.maximum(i - 1, 0)])


def _moe_up_kernel(be_ref, nu_ref, x_ref, wg_ref, wu_ref, bg_ref, bu_ref, o_ref, wg_bf, wu_bf):
    i = pl.program_id(1)

    @pl.when(i < nu_ref[0])
    def _():
        @pl.when(_expert_changed(i, be_ref))
        def _():
            wg_bf[...] = wg_ref[0].astype(BF16)
            wu_bf[...] = wu_ref[0].astype(BF16)

        o_ref[...] = _swiglu(x_ref[...], wg_bf, wu_bf, bg_ref, bu_ref).astype(o_ref.dtype)

    @pl.when(i >= nu_ref[0])
    def _():
        o_ref[...] = jnp.zeros_like(o_ref)


def _swiglu(x, wg_bf, wu_bf, bg_ref, bu_ref):
    gate = jnp.dot(x, wg_bf[...], preferred_element_type=F32) + bg_ref[0]
    up = jnp.dot(x, wu_bf[...], preferred_element_type=F32) + bu_ref[0]
    gate = jnp.minimum(gate, SWIGLU_LIMIT)
    up = jnp.clip(up, -SWIGLU_LIMIT, SWIGLU_LIMIT)
    return gate * jax.nn.sigmoid(SWIGLU_ALPHA * gate) * (up + 1.0)


def _moe_up_gather_kernel(tok_ref, be_ref, nu_ref, src_ref, wg_ref, wu_ref, bg_ref, bu_ref,
                          o_ref, xs_ref, wg_bf, wu_bf, buf0, buf1, buf2, sem0, sem1, sem2, *, tm):
    i = pl.program_id(0)
    nu = nu_ref[0]
    slots = ((buf0, sem0), (buf1, sem1), (buf2, sem2))

    def issue(blk, buf, sem):
        for r in range(tm):
            t = tok_ref[blk * tm + r]
            pltpu.make_async_copy(src_ref.at[pl.ds(t, 1)], buf.at[pl.ds(r, 1)], sem).start()

    def wait(buf, sem):
        pltpu.make_async_copy(src_ref.at[pl.ds(0, tm)], buf, sem).wait()

    @pl.when(i == 0)
    def _():
        issue(0, buf0, sem0)
        issue(1, buf1, sem1)

    def work(buf, sem, next_buf, next_sem):
        @pl.when(_expert_changed(i, be_ref))
        def _():
            wg_bf[...] = wg_ref[0].astype(BF16)
            wu_bf[...] = wu_ref[0].astype(BF16)

        wait(buf, sem)
        x = buf[...].astype(BF16)
        xs_ref[...] = x
        o_ref[...] = _swiglu(x, wg_bf, wu_bf, bg_ref, bu_ref).astype(o_ref.dtype)
        issue(i + 2, next_buf, next_sem)

    for s in range(3):
        @pl.when(jnp.logical_and(i < nu, i % 3 == s))
        def _(s=s):
            work(*slots[s], *slots[(s + 2) % 3])

    @pl.when(i >= nu)
    def _():
        o_ref[...] = jnp.zeros_like(o_ref)
        xs_ref[...] = jnp.zeros_like(xs_ref)

    for s in range(3):
        @pl.when(jnp.logical_and(i == nu, nu % 3 == s))
        def _(s=s):
            wait(*slots[s])
            wait(*slots[(s + 1) % 3])


def moe_up_gather(rows_tok, blk_expert, n_used, src, w_gate, b_gate, w_up, b_up, tm, tn=1024):
    n_rows = rows_tok.shape[0]
    D = src.shape[1]
    E, _, F = w_gate.shape
    w_spec = pl.BlockSpec((1, D, tn), lambda i, tok, be, nu: (be[_blk(i, nu)], 0, 0))
    b_spec = pl.BlockSpec((1, 1, tn), lambda i, tok, be, nu: (be[_blk(i, nu)], 0, 0))
    return pl.pallas_call(
        functools.partial(_moe_up_gather_kernel, tm=tm),
        grid_spec=pltpu.PrefetchScalarGridSpec(
            num_scalar_prefetch=3,
            grid=(n_rows // tm,),
            in_specs=[pl.BlockSpec(memory_space=pl.ANY), w_spec, w_spec, b_spec, b_spec],
            out_specs=[pl.BlockSpec((tm, tn), lambda i, tok, be, nu: (i, 0)),
                       pl.BlockSpec((tm, D), lambda i, tok, be, nu: (i, 0))],
            scratch_shapes=[pltpu.VMEM((D, tn), BF16), pltpu.VMEM((D, tn), BF16),
                            pltpu.VMEM((tm, D), F32), pltpu.VMEM((tm, D), F32), pltpu.VMEM((tm, D), F32),
                            pltpu.SemaphoreType.DMA(()), pltpu.SemaphoreType.DMA(()),
                            pltpu.SemaphoreType.DMA(())],
        ),
        out_shape=[jax.ShapeDtypeStruct((n_rows, tn), BF16), jax.ShapeDtypeStruct((n_rows, D), BF16)],
        compiler_params=_params(("arbitrary",), 56, row_dma=True),
        name="moe_up_gather",
    )(rows_tok, blk_expert, n_used, src, w_gate, w_up, b_gate.reshape(E, 1, F), b_up.reshape(E, 1, F))


def _moe_down_kernel(be_ref, nu_ref, h0_ref, h1_ref, wd_ref, bd_ref, o_ref, wd_bf):
    i = pl.program_id(1)
    F0 = h0_ref.shape[1]

    @pl.when(i < nu_ref[0])
    def _():
        @pl.when(_expert_changed(i, be_ref))
        def _():
            wd_bf[...] = wd_ref[0].astype(BF16)

        o_ref[...] = (jnp.dot(h0_ref[...], wd_bf[:F0, :], preferred_element_type=F32)
                      + jnp.dot(h1_ref[...], wd_bf[F0:, :], preferred_element_type=F32) + bd_ref[0])

    @pl.when(i >= nu_ref[0])
    def _():
        o_ref[...] = jnp.zeros_like(o_ref)


def _blk(i, nu):
    return jnp.minimum(i, nu[0] - 1)


def moe_up(blk_expert, n_used, xs, w_gate, b_gate, w_up, b_up, tm, tn=1024):
    n_rows, D = xs.shape
    E, _, F = w_gate.shape
    w_spec = pl.BlockSpec((1, D, tn), lambda j, i, be, nu: (be[_blk(i, nu)], 0, j + 1))
    b_spec = pl.BlockSpec((1, 1, tn), lambda j, i, be, nu: (be[_blk(i, nu)], 0, j + 1))
    return pl.pallas_call(
        _moe_up_kernel,
        grid_spec=pltpu.PrefetchScalarGridSpec(
            num_scalar_prefetch=2,
            grid=(F // tn - 1, n_rows // tm),
            in_specs=[pl.BlockSpec((tm, D), lambda j, i, be, nu: (_blk(i, nu), 0)),
                      w_spec, w_spec, b_spec, b_spec],
            out_specs=pl.BlockSpec((tm, tn), lambda j, i, be, nu: (i, j)),
            scratch_shapes=[pltpu.VMEM((D, tn), BF16), pltpu.VMEM((D, tn), BF16)],
        ),
        out_shape=jax.ShapeDtypeStruct((n_rows, F - tn), BF16),
        compiler_params=_params(("arbitrary", "arbitrary"), 56),
        name="moe_up",
    )(blk_expert, n_used, xs, w_gate, w_up, b_gate.reshape(E, 1, F), b_up.reshape(E, 1, F))


def moe_down(blk_expert, n_used, hs0, hs1, w_down, b_down, tm, tn=2048):
    n_rows, F0 = hs0.shape
    F1 = hs1.shape[1]
    F = F0 + F1
    E, _, D = w_down.shape
    tn = min(tn, D)
    return pl.pallas_call(
        _moe_down_kernel,
        grid_spec=pltpu.PrefetchScalarGridSpec(
            num_scalar_prefetch=2,
            grid=(D // tn, n_rows // tm),
            in_specs=[pl.BlockSpec((tm, F0), lambda j, i, be, nu: (_blk(i, nu), 0)),
                      pl.BlockSpec((tm, F1), lambda j, i, be, nu: (_blk(i, nu), 0)),
                      pl.BlockSpec((1, F, tn), lambda j, i, be, nu: (be[_blk(i, nu)], 0, j)),
                      pl.BlockSpec((1, 1, tn), lambda j, i, be, nu: (be[_blk(i, nu)], 0, j))],
            out_specs=pl.BlockSpec((tm, tn), lambda j, i, be, nu: (i, j)),
            scratch_shapes=[pltpu.VMEM((F, tn), BF16)],
        ),
        out_shape=jax.ShapeDtypeStruct((n_rows, D), F32),
        compiler_params=_params(("arbitrary", "arbitrary"), 56),
        name="moe_down",
    )(blk_expert, n_used, hs0, hs1, w_down, b_down.reshape(E, 1, D))


def _combine_kernel(dest_ref, y_ref, p_ref, h_ref, g_ref, o_ref, buf0, buf1, sem0, sem1, *,
                    tc, final_norm):
    i = pl.program_id(0)

    def issue(blk, buf, sem):
        def rows(r, c):
            for k in range(TOP_K):
                d = dest_ref[(blk * tc + r) * TOP_K + k]
                pltpu.make_async_copy(y_ref.at[pl.ds(d, 1)], buf.at[k, pl.ds(r, 1)], sem).start()
            return c

        lax.fori_loop(0, tc, rows, 0, unroll=4)

    @pl.when(i == 0)
    def _():
        issue(0, buf0, sem0)

    def work(buf, sem, next_buf, next_sem):
        @pl.when(i + 1 < pl.num_programs(0))
        def _():
            issue(i + 1, next_buf, next_sem)

        for k in range(TOP_K):
            pltpu.make_async_copy(y_ref.at[pl.ds(0, tc)], buf.at[k], sem).wait()
        p = p_ref[...]
        h = h_ref[...]
        for k in range(TOP_K):
            h = h + p[:, k:k + 1] * buf[k]
        o_ref[...] = _rms(h, g_ref[...]) if final_norm else h

    @pl.when(i % 2 == 0)
    def _():
        work(buf0, sem0, buf1, sem1)

    @pl.when(i % 2 == 1)
    def _():
        work(buf1, sem1, buf0, sem0)


def combine(dest_flat, y, probs, h, g_final, final_norm, tc=128):
    T, D = h.shape
    return pl.pallas_call(
        functools.partial(_combine_kernel, tc=tc, final_norm=final_norm),
        grid_spec=pltpu.PrefetchScalarGridSpec(
            num_scalar_prefetch=1,
            grid=(T // tc,),
            in_specs=[pl.BlockSpec(memory_space=pl.ANY),
                      pl.BlockSpec((tc, LANES), lambda i, d: (i, 0)),
                      pl.BlockSpec((tc, D), lambda i, d: (i, 0)),
                      pl.BlockSpec((1, D), lambda i, d: (0, 0))],
            out_specs=pl.BlockSpec((tc, D), lambda i, d: (i, 0)),
            scratch_shapes=[pltpu.VMEM((TOP_K, tc, D), F32), pltpu.VMEM((TOP_K, tc, D), F32),
                            pltpu.SemaphoreType.DMA(()), pltpu.SemaphoreType.DMA(())],
        ),
        out_shape=jax.ShapeDtypeStruct((T, D), F32),
        compiler_params=_params(("arbitrary",), 32, row_dma=True),
        name="combine",
    )(dest_flat, y, probs, h, g_final.reshape(1, D))


MOE_ROW_BLOCK = 256


def kernel(x, mem, norm_mix_g, norm_mem_g, w_in, w_pool_group, pool_scale, w_pool_proj, w_sb_proj,
           w_mem_kv, w_mem_proj, w_out, norm_ffn_g, w_router, b_router, w_gate, b_gate, w_up, b_up,
           w_down, b_down, norm_final_g):
    B, S, D = x.shape
    M = mem.shape[1]
    T = B * S
    depth = w_in.shape[0]
    pool_w = w_pool_proj.shape[1]
    sb_w = w_sb_proj.shape[1]
    xa_w = w_mem_proj.shape[1]
    E = w_router.shape[2]
    sb_heads = sb_w // HEAD_DIM
    xa_heads = xa_w // HEAD_DIM
    qkv_w = 3 * sb_w + xa_w
    assert (3 * sb_w) % xa_w == 0

    h = x.reshape(T, D)
    mem2 = mem.reshape(B * M, D)
    for l in range(depth):
        w_in_bf = w_in[l].astype(BF16)
        u_pool = norm_proj(h, norm_mix_g[l], w_in_bf[:, :pool_w], F32)
        qkv = norm_proj(h, norm_mix_g[l], w_in_bf[:, pool_w:pool_w + qkv_w], BF16, tn=qkv_w // 2)
        gates = norm_proj(h, norm_mix_g[l], w_in_bf[:, pool_w + qkv_w:], F32, act="sigmoid")
        memkv = norm_proj(mem2, norm_mem_g[l], w_mem_kv[l].astype(BF16), BF16)
        p = pool_mixer(u_pool, w_pool_group[l], pool_scale[l], B, S)
        a_sb = sb_attention(qkv, B, S, sb_heads)
        a_xa = mem_attention(qkv, memkv, S, (3 * sb_w) // xa_w, xa_heads)
        h, hn, logits = merge_block(p, a_sb, a_xa, gates, h, w_pool_proj[l], w_sb_proj[l],
                                    w_mem_proj[l], w_out[l], norm_ffn_g[l], w_router[l], b_router[l])
        idx, probs, rank, counts = route(logits, E)
        tm = MOE_ROW_BLOCK
        n_blk = (T * TOP_K) // tm + E + 2
        counts = counts[0, :E].astype(jnp.int32)
        padded = (counts + tm - 1) // tm * tm
        pad_end = jnp.cumsum(padded)
        pad_start = pad_end - padded
        dest = (pad_start[idx[:, :TOP_K]] + rank[:, :TOP_K]).reshape(-1)
        rows_tok = invert_routing(dest, n_blk * tm)
        blk_start = jnp.arange(n_blk, dtype=jnp.int32) * tm
        blk_expert = jnp.minimum(
            jnp.sum((pad_end[None, :] <= blk_start[:, None]).astype(jnp.int32), axis=1), E - 1)
        n_used = (pad_end[-1:] // tm).astype(jnp.int32)
        hs0, xs = moe_up_gather(rows_tok, blk_expert, n_used, hn, w_gate[l], b_gate[l], w_up[l],
                                b_up[l], tm)
        hs1 = moe_up(blk_expert, n_used, xs, w_gate[l], b_gate[l], w_up[l], b_up[l], tm)
        ys = moe_down(blk_expert, n_used, hs0, hs1, w_down[l], b_down[l], tm)
        h = combine(dest, ys, probs, h, norm_final_g,
                    final_norm=(l == depth - 1))
    return h.reshape(B, S, D)
```
